```python
import jax, jax.numpy as jnp
from jax import lax
import numpy as np

D_MODEL = 1024
BATCH = 8
SEQ = 8192
DEPTH = 1
DEC_BATCH = 128
DEC_SEQ = 4
PAST_LEN = 8192
PAGE_SIZE = 128

HEAD_DIM = 64
HEADS_PER_GROUP = 8
WINDOWS = (128, 512, 2048)
DILATIONS = (1, 4, 16)
N_GROUPS = 3
ATTN_WIDTH = HEADS_PER_GROUP * HEAD_DIM
ATTN_BLOCK = 128
QKV_COLS = 3 * ATTN_WIDTH
CONV_CH = D_MODEL // 2
CONV_WIDTH = 31
IN_COLS = N_GROUPS * QKV_COLS + 2 * CONV_CH + 2 * D_MODEL
N_EXPERTS = 32
TOP_K = 4
D_FF = D_MODEL
SWIGLU_ALPHA = 1.702
SWIGLU_LIMIT = 7.0
EXPERT_BLOCK = 256
NORM_EPS = 1e-6
NEG_INF = -1e30

kernel_name = 'dilated_attn_conformer_moe_hybrid_step'


def rms_norm(x, gain):
    xf = x.astype(jnp.float32)
    y = xf * lax.rsqrt(jnp.mean(xf * xf, axis=-1, keepdims=True) + NORM_EPS)
    return (y * gain.astype(jnp.float32)).astype(x.dtype)


def layer_norm(x, gain, bias):
    xf = x.astype(jnp.float32)
    mu = jnp.mean(xf, axis=-1, keepdims=True)
    xc = xf - mu
    y = xc * lax.rsqrt(jnp.mean(xc * xc, axis=-1, keepdims=True) + NORM_EPS)
    return (y * gain.astype(jnp.float32) + bias.astype(jnp.float32)).astype(x.dtype)


def alibi_slopes():
    n = N_GROUPS * HEADS_PER_GROUP
    s = 2.0 ** (-8.0 * jnp.arange(1, n + 1, dtype=jnp.float32) / n)
    return s.reshape(N_GROUPS, HEADS_PER_GROUP)


def mixer_inputs(x, norm_gain, w_in, q_gain, k_gain):
    B, T, _ = x.shape
    h = rms_norm(x, norm_gain)
    z = jnp.einsum('btd,dc->btc', h, w_in)
    qkv_all, glu_in, gate_in = jnp.split(z, [N_GROUPS * QKV_COLS, N_GROUPS * QKV_COLS + 2 * CONV_CH], axis=-1)
    qkv = qkv_all.reshape(B, T, N_GROUPS, 3, HEADS_PER_GROUP, HEAD_DIM)
    q = rms_norm(qkv[:, :, :, 0], q_gain)
    k = rms_norm(qkv[:, :, :, 1], k_gain)
    v = qkv[:, :, :, 2]
    a, g = jnp.split(glu_in, 2, axis=-1)
    u = a * jax.nn.sigmoid(g)
    return q, k, v, u, gate_in


def dilated_attention_prompt(q, k, v, dilation, window, slopes):
    B, S, H, Dh = q.shape
    span = dilation * ATTN_BLOCK
    Sp = -(-S // span) * span
    NB = Sp // span
    n_back = window // dilation

    def to_blocks(a):
        a = jnp.pad(a.astype(jnp.float32), ((0, 0), (0, Sp - S), (0, 0), (0, 0)))
        return a.reshape(B, NB, ATTN_BLOCK, dilation, H, Dh).transpose(0, 3, 1, 2, 4, 5)

    def with_prev(a):
        prev = jnp.pad(a, ((0, 0), (0, 0), (1, 0), (0, 0), (0, 0), (0, 0)))[:, :, :-1]
        return jnp.concatenate([prev, a], axis=3)

    qb = to_blocks(q)
    kb = with_prev(to_blocks(k))
    vb = with_prev(to_blocks(v))
    s = jnp.einsum('brnqhc,brnkhc->brnhqk', qb, kb) * (HEAD_DIM ** -0.5)
    i = jnp.arange(ATTN_BLOCK)[:, None]
    j = jnp.arange(2 * ATTN_BLOCK)[None, :]
    steps = i + ATTN_BLOCK - j
    u_q = jnp.arange(NB)[:, None, None] * ATTN_BLOCK + i[None]
    valid = (steps >= 0) & (steps <= n_back) & (u_q - steps >= 0)
    bias = -slopes[:, None, None] * (steps * dilation).astype(jnp.float32)
    s = jnp.where(valid[:, None], s + bias, NEG_INF)
    m = jnp.max(s, axis=-1, keepdims=True)
    p = jnp.exp(s - m)
    l = jnp.sum(p, axis=-1, keepdims=True)
    o = jnp.einsum('brnhqk,brnkhc->brnqhc', p, vb) / jnp.swapaxes(l, 3, 4)
    log_denom = (m + jnp.log(l))[..., 0]
    o = o.transpose(0, 2, 3, 1, 4, 5).reshape(B, Sp, H, Dh)[:, :S]
    log_denom = log_denom.transpose(0, 2, 4, 1, 3).reshape(B, Sp, H)[:, :S]
    return o, log_denom


def dilated_attention_sample(q, k_ctx, v_ctx, dilation, window, slopes):
    T = q.shape[1]
    L = k_ctx.shape[1] - T
    n_back = window // dilation
    steps = jnp.arange(n_back + 1)
    idx = L + jnp.arange(T)[:, None] - steps[None, :] * dilation
    valid = idx >= 0
    idx = jnp.maximum(idx, 0)
    kg = k_ctx[:, idx].astype(jnp.float32)
    vg = v_ctx[:, idx].astype(jnp.float32)
    s = jnp.einsum('bthc,btkhc->bthk', q.astype(jnp.float32), kg) * (HEAD_DIM ** -0.5)
    bias = -slopes[:, None] * (steps * dilation).astype(jnp.float32)
    s = jnp.where(valid[:, None, :], s + bias, NEG_INF)
    m = jnp.max(s, axis=-1, keepdims=True)
    p = jnp.exp(s - m)
    l = jnp.sum(p, axis=-1, keepdims=True)
    o = jnp.einsum('bthk,btkhc->bthc', p, vg) / l
    return o, (m + jnp.log(l))[..., 0]


def combine_groups(outs, log_denoms):
    w = jax.nn.softmax(jnp.stack(log_denoms, axis=0), axis=0)
    return jnp.sum(w[..., None] * jnp.stack(outs, axis=0), axis=0)


def conformer_conv(u_ctx, conv_w, conv_b, ln_gain, ln_bias):
    y = lax.conv_general_dilated(u_ctx, conv_w[:, None, :], window_strides=(1,), padding='VALID',
                                 dimension_numbers=('NWC', 'WIO', 'NWC'), feature_group_count=CONV_CH)
    y = layer_norm(y + conv_b, ln_gain, ln_bias)
    return jax.nn.silu(y)


def expert_ffn(xb, w_gu, b_gu, w_dn, b_dn):
    gu = (xb @ w_gu + b_gu).astype(jnp.float32)
    glu, lin = jnp.split(gu, 2, axis=-1)
    glu = jnp.minimum(glu, SWIGLU_LIMIT)
    lin = jnp.clip(lin, -SWIGLU_LIMIT, SWIGLU_LIMIT)
    act = glu * jax.nn.sigmoid(SWIGLU_ALPHA * glu) * (lin + 1.0)
    return act.astype(xb.dtype) @ w_dn + b_dn


def moe_sublayer(x, norm_gain, w_router, b_router, w_gate_up, b_gate_up, w_down, b_down):
    B, T, D = x.shape
    N = B * T
    h = rms_norm(x, norm_gain).reshape(N, D)
    logits = h.astype(jnp.float32) @ w_router.astype(jnp.float32) + b_router.astype(jnp.float32)
    top_val, top_idx = lax.top_k(logits, TOP_K)
    gate = jax.nn.softmax(top_val, axis=-1)
    M = N * TOP_K
    e_flat = top_idx.reshape(M)
    tok_flat = jnp.arange(M, dtype=jnp.int32) // TOP_K
    g_flat = gate.reshape(M)
    order = jnp.argsort(e_flat)
    e_s, tok_s, g_s = e_flat[order], tok_flat[order], g_flat[order]
    counts = jnp.bincount(e_flat, length=N_EXPERTS)
    starts = jnp.cumsum(counts) - counts
    pcounts = (counts + EXPERT_BLOCK - 1) // EXPERT_BLOCK * EXPERT_BLOCK
    pends = jnp.cumsum(pcounts)
    pstarts = pends - pcounts
    dest = pstarts[e_s] + jnp.arange(M) - starts[e_s]
    n_blocks = -(-M // EXPERT_BLOCK) + N_EXPERTS
    R = n_blocks * EXPERT_BLOCK
    row_tok = jnp.zeros((R,), jnp.int32).at[dest].set(tok_s)
    row_gate = jnp.zeros((R,), jnp.float32).at[dest].set(g_s)
    block_expert = jnp.minimum(jnp.searchsorted(pends, jnp.arange(n_blocks) * EXPERT_BLOCK, side='right'), N_EXPERTS - 1)
    xb = h[row_tok].reshape(n_blocks, EXPERT_BLOCK, D)

    def run_block(args):
        xblk, e = args
        return expert_ffn(xblk, w_gate_up[e], b_gate_up[e], w_down[e], b_down[e])

    yb = lax.map(run_block, (xb, block_expert)).reshape(R, D)
    y = jnp.zeros((N, D), jnp.float32).at[row_tok].add(yb.astype(jnp.float32) * row_gate[:, None])
    return x + y.reshape(B, T, D).astype(x.dtype)


def finish_layer(x, y_attn, y_conv, gate_in, lw):
    ga, gc = jnp.split(gate_in, 2, axis=-1)
    ya = jnp.einsum('bte,ed->btd', y_attn, lw['w_attn_branch'])
    yc = jnp.einsum('btc,cd->btd', y_conv, lw['w_conv_branch'])
    mixed = jax.nn.sigmoid(ga + lw['b_gate'][0]) * ya + jax.nn.sigmoid(gc + lw['b_gate'][1]) * yc
    x = x + jnp.einsum('btd,de->bte', mixed, lw['w_out'])
    return moe_sublayer(x, lw['norm2_gain'], lw['w_router'], lw['b_router'], lw['w_gate_up'],
                        lw['b_gate_up'], lw['w_down'], lw['b_down'])


def prompt_layer(x, lw):
    B, S, _ = x.shape
    q, k, v, u, gate_in = mixer_inputs(x, lw['norm1_gain'], lw['w_in'], lw['q_norm_gain'], lw['k_norm_gain'])
    slopes = alibi_slopes()
    outs, lds, kv_new = [], [], []
    for g in range(N_GROUPS):
        o, ld = dilated_attention_prompt(q[:, :, g], k[:, :, g], v[:, :, g], DILATIONS[g], WINDOWS[g], slopes[g])
        outs.append(o)
        lds.append(ld)
        keep = min(WINDOWS[g], S)
        kv_new.append(jnp.stack([k[:, S - keep:, g], v[:, S - keep:, g]], axis=2))
    y_attn = combine_groups(outs, lds).reshape(B, S, ATTN_WIDTH).astype(x.dtype)
    u_ctx = jnp.pad(u, ((0, 0), (CONV_WIDTH - 1, 0), (0, 0)))
    y_conv = conformer_conv(u_ctx, lw['conv_w'], lw['conv_b'], lw['conv_ln_gain'], lw['conv_ln_bias'])
    conv_new = u[:, S - (CONV_WIDTH - 1):]
    return finish_layer(x, y_attn, y_conv, gate_in, lw), kv_new, conv_new


def sample_layer(x, kv_bufs, conv_buf, lw):
    B, T, _ = x.shape
    q, k, v, u, gate_in = mixer_inputs(x, lw['norm1_gain'], lw['w_in'], lw['q_norm_gain'], lw['k_norm_gain'])
    slopes = alibi_slopes()
    outs, lds, kv_new = [], [], []
    for g in range(N_GROUPS):
        buf = kv_bufs[g]
        L = buf.shape[1]
        k_ctx = jnp.concatenate([buf[:, :, 0], k[:, :, g]], axis=1)
        v_ctx = jnp.concatenate([buf[:, :, 1], v[:, :, g]], axis=1)
        o, ld = dilated_attention_sample(q[:, :, g], k_ctx, v_ctx, DILATIONS[g], WINDOWS[g], slopes[g])
        outs.append(o)
        lds.append(ld)
        kv_new.append(jnp.stack([k_ctx[:, -L:], v_ctx[:, -L:]], axis=2))
    y_attn = combine_groups(outs, lds).reshape(B, T, ATTN_WIDTH).astype(x.dtype)
    u_ctx = jnp.concatenate([conv_buf.astype(u.dtype), u], axis=1)
    y_conv = conformer_conv(u_ctx, lw['conv_w'], lw['conv_b'], lw['conv_ln_gain'], lw['conv_ln_bias'])
    conv_new = u_ctx[:, -(CONV_WIDTH - 1):]
    return finish_layer(x, y_attn, y_conv, gate_in, lw), kv_new, conv_new


def setup_inputs(seed: int = 0) -> dict:
    key = jax.random.key(seed)
    ks = jax.random.split(key, 32)
    f32 = jnp.float32

    def nrm(k, shape, scale):
        return scale * jax.random.normal(k, shape, f32)

    return {
        'x_prompt': nrm(ks[0], (BATCH, SEQ, D_MODEL), 1.0),
        'x_sample': nrm(ks[1], (DEC_BATCH, DEC_SEQ, D_MODEL), 1.0),
        'cache_kv_w128': nrm(ks[2], (DEPTH, DEC_BATCH, min(WINDOWS[0], PAST_LEN), 2, HEADS_PER_GROUP, HEAD_DIM), 1.0),
        'cache_kv_w512': nrm(ks[3], (DEPTH, DEC_BATCH, min(WINDOWS[1], PAST_LEN), 2, HEADS_PER_GROUP, HEAD_DIM), 1.0),
        'cache_kv_w2048': nrm(ks[4], (DEPTH, DEC_BATCH, min(WINDOWS[2], PAST_LEN), 2, HEADS_PER_GROUP, HEAD_DIM), 1.0),
        'state_conv': nrm(ks[5], (DEPTH, DEC_BATCH, CONV_WIDTH - 1, CONV_CH), 0.5),
        'norm1_gain': 1.0 + nrm(ks[6], (DEPTH, D_MODEL), 0.05),
        'w_in': nrm(ks[7], (DEPTH, D_MODEL, IN_COLS), D_MODEL ** -0.5),
        'q_norm_gain': 1.0 + nrm(ks[8], (DEPTH, HEAD_DIM), 0.05),
        'k_norm_gain': 1.0 + nrm(ks[9], (DEPTH, HEAD_DIM), 0.05),
        'b_gate': nrm(ks[10], (DEPTH, 2, D_MODEL), 0.02),
        'conv_w': nrm(ks[11], (DEPTH, CONV_WIDTH, CONV_CH), CONV_WIDTH ** -0.5),
        'conv_b': nrm(ks[12], (DEPTH, CONV_CH), 0.02),
        'conv_ln_gain': 1.0 + nrm(ks[13], (DEPTH, CONV_CH), 0.05),
        'conv_ln_bias': nrm(ks[14], (DEPTH, CONV_CH), 0.02),
        'w_attn_branch': nrm(ks[15], (DEPTH, ATTN_WIDTH, D_MODEL), ATTN_WIDTH ** -0.5),
        'w_conv_branch': nrm(ks[16], (DEPTH, CONV_CH, D_MODEL), CONV_CH ** -0.5),
        'w_out': nrm(ks[17], (DEPTH, D_MODEL, D_MODEL), D_MODEL ** -0.5),
        'norm2_gain': 1.0 + nrm(ks[18], (DEPTH, D_MODEL), 0.05),
        'w_router': nrm(ks[19], (DEPTH, D_MODEL, N_EXPERTS), D_MODEL ** -0.5),
        'b_router': nrm(ks[20], (DEPTH, N_EXPERTS), 0.01),
        'w_gate_up': nrm(ks[21], (DEPTH, N_EXPERTS, D_MODEL, 2 * D_FF), D_MODEL ** -0.5),
        'b_gate_up': nrm(ks[22], (DEPTH, N_EXPERTS, 2 * D_FF), 0.02),
        'w_down': nrm(ks[23], (DEPTH, N_EXPERTS, D_FF, D_MODEL), D_FF ** -0.5),
        'b_down': nrm(ks[24], (DEPTH, N_EXPERTS, D_MODEL), 0.02),
    }


def reference(x_prompt, x_sample, cache_kv_w128, cache_kv_w512, cache_kv_w2048, state_conv,
              norm1_gain, w_in, q_norm_gain, k_norm_gain, b_gate, conv_w, conv_b, conv_ln_gain,
              conv_ln_bias, w_attn_branch, w_conv_branch, w_out, norm2_gain, w_router, b_router,
              w_gate_up, b_gate_up, w_down, b_down):
    kv_caches = (cache_kv_w128, cache_kv_w512, cache_kv_w2048)
    x_p, x_s = x_prompt, x_sample
    kv_p = [[] for _ in range(N_GROUPS)]
    kv_s = [[] for _ in range(N_GROUPS)]
    conv_p, conv_s = [], []
    for l in range(DEPTH):
        lw = dict(norm1_gain=norm1_gain[l], w_in=w_in[l], q_norm_gain=q_norm_gain[l],
                  k_norm_gain=k_norm_gain[l], b_gate=b_gate[l], conv_w=conv_w[l], conv_b=conv_b[l],
                  conv_ln_gain=conv_ln_gain[l], conv_ln_bias=conv_ln_bias[l],
                  w_attn_branch=w_attn_branch[l], w_conv_branch=w_conv_branch[l], w_out=w_out[l],
                  norm2_gain=norm2_gain[l], w_router=w_router[l], b_router=b_router[l],
                  w_gate_up=w_gate_up[l], b_gate_up=b_gate_up[l], w_down=w_down[l], b_down=b_down[l])
        x_p, kvp_l, cp_l = prompt_layer(x_p, lw)
        x_s, kvs_l, cs_l = sample_layer(x_s, [c[l] for c in kv_caches], state_conv[l], lw)
        for g in range(N_GROUPS):
            kv_p[g].append(kvp_l[g])
            kv_s[g].append(kvs_l[g])
        conv_p.append(cp_l)
        conv_s.append(cs_l)
    return (x_p, x_s,
            jnp.stack(kv_p[0]), jnp.stack(kv_p[1]), jnp.stack(kv_p[2]), jnp.stack(conv_p),
            jnp.stack(kv_s[0]), jnp.stack(kv_s[1]), jnp.stack(kv_s[2]), jnp.stack(conv_s))
```

```python
import functools

import jax
import jax.numpy as jnp
from jax import lax
from jax.experimental import pallas as pl
from jax.experimental.pallas import tpu as pltpu

F32 = jnp.float32
BF16 = jnp.bfloat16

D_MODEL = 1024
HEAD_DIM = 64
HEADS = 8
ATTN_WIDTH = HEADS * HEAD_DIM
N_GROUPS = 3
WINDOWS = (128, 512, 2048)
DILATIONS = (1, 4, 16)
ATTN_BLOCK = 128
QKV_COLS = 3 * ATTN_WIDTH
ALL_QKV = N_GROUPS * QKV_COLS
CONV_CH = 512
CONV_WIDTH = 31
CONV_HALO = 32
N_EXPERTS = 32
TOP_K = 4
D_FF = 1024
SWIGLU_ALPHA = 1.702
SWIGLU_LIMIT = 7.0
NORM_EPS = 1e-6
NEG_INF = -1e30

LANES = 128
MXU_DIM = 256
VMEM_LIMIT = 56 * 1024 * 1024

IN_TILE = 512
ATTN_TILE = 512
CONV_TILE = 256
CONV_CHUNK = 64
FIN_TILE = 256
MOE_TILE = 1536
MOE_CHUNK = 256


def _sigmoid(x):
    return 1.0 / (1.0 + jnp.exp(-x))


def _rms(x, gain):
    ms = jnp.mean(x * x, axis=-1, keepdims=True)
    return x * lax.rsqrt(ms + NORM_EPS) * gain


def _params(*sem):
    return pltpu.CompilerParams(dimension_semantics=sem, vmem_limit_bytes=VMEM_LIMIT)


N_QKV_TILES = ALL_QKV // ATTN_WIDTH
N_IN_TILES = N_QKV_TILES + 2


def _inproj_kernel(x_ref, g1_ref, w_ref, qk_ref, bd_ref, z_ref, u_ref, h_scr, a_scr):
    j = pl.program_id(1)

    @pl.when(j == 0)
    def _():
        h_scr[...] = _rms(x_ref[...], g1_ref[...]).astype(BF16)

    z = jnp.dot(h_scr[...], w_ref[...], preferred_element_type=F32)
    part = j % 3

    @pl.when((j < N_QKV_TILES) & (part == 2))
    def _():
        z_ref[...] = z

    @pl.when((j < N_QKV_TILES) & (part != 2))
    def _():
        zz = (z * z).astype(BF16)
        bd = bd_ref[...]
        ss = jnp.concatenate(
            [jnp.dot(zz[:, :MXU_DIM], bd, preferred_element_type=F32),
             jnp.dot(zz[:, MXU_DIM:], bd, preferred_element_type=F32)], axis=1)
        gain = qk_ref[pl.ds(part, 1), :]
        z_ref[...] = z * lax.rsqrt(ss * (1.0 / HEAD_DIM) + NORM_EPS) * gain

    @pl.when(j == N_QKV_TILES)
    def _():
        a_scr[...] = z

    @pl.when(j == N_QKV_TILES + 1)
    def _():
        u_ref[...] = a_scr[...] * _sigmoid(z)


def _in_projection(x2d, g1, w_bf, qk_gain, bd):
    n = x2d.shape[0]
    tm = min(IN_TILE, n)
    return pl.pallas_call(
        _inproj_kernel,
        grid=(n // tm, N_IN_TILES),
        in_specs=[
            pl.BlockSpec((tm, D_MODEL), lambda i, j: (i, 0)),
            pl.BlockSpec((1, D_MODEL), lambda i, j: (0, 0)),
            pl.BlockSpec((D_MODEL, ATTN_WIDTH), lambda i, j: (0, j)),
            pl.BlockSpec((8, ATTN_WIDTH), lambda i, j: (0, 0)),
            pl.BlockSpec((MXU_DIM, MXU_DIM), lambda i, j: (0, 0)),
        ],
        out_specs=[
            pl.BlockSpec((tm, ATTN_WIDTH), lambda i, j: (i, jnp.minimum(j, N_QKV_TILES - 1))),
            pl.BlockSpec((tm, CONV_CH), lambda i, j: (i, 0)),
        ],
        out_shape=[jax.ShapeDtypeStruct((n, ALL_QKV), F32),
                   jax.ShapeDtypeStruct((n, CONV_CH), F32)],
        scratch_shapes=[pltpu.VMEM((tm, D_MODEL), BF16), pltpu.VMEM((tm, CONV_CH), F32)],
        compiler_params=_params("parallel", "arbitrary"),
        name="in_projection",
    )(x2d, g1, w_bf, qk_gain, bd)


def _attn_kernel(q_ref, kc_ref, vc_ref, kp_ref, vp_ref, bias_ref, o_ref, ld_ref, kf, vf, *, tq):
    n = pl.program_id(2)
    kf[0:ATTN_BLOCK, :] = kp_ref[0].astype(BF16)
    kf[ATTN_BLOCK:, :] = kc_ref[0].astype(BF16)
    vf[0:ATTN_BLOCK, :] = vp_ref[0].astype(BF16)
    vf[ATTN_BLOCK:, :] = vc_ref[0].astype(BF16)
    lo = lax.broadcasted_iota(jnp.int32, (ATTN_BLOCK, LANES), 1) < HEAD_DIM
    for i in range(tq // ATTN_BLOCK):
        rows = slice(i * ATTN_BLOCK, (i + 1) * ATTN_BLOCK)
        krows = slice(i * ATTN_BLOCK, (i + 2) * ATTN_BLOCK)
        table = jnp.where(n == 0, 0, 1) if i == 0 else 1
        for hp in range(HEADS // 2):
            cols = slice(hp * LANES, (hp + 1) * LANES)
            qp = q_ref[0, rows, cols] * (HEAD_DIM ** -0.5)
            qq = jnp.concatenate([jnp.where(lo, qp, 0.0), jnp.where(lo, 0.0, qp)], axis=0).astype(BF16)
            s = lax.dot_general(qq, kf[krows, cols], (((1,), (1,)), ((), ())),
                                preferred_element_type=F32)
            s = s + bias_ref[table, hp]
            m = jnp.max(s, axis=-1, keepdims=True)
            p = jnp.exp(s - m)
            l = jnp.sum(p, axis=-1, keepdims=True)
            o2 = jnp.dot(p.astype(BF16), vf[krows, cols], preferred_element_type=F32) / l
            ld2 = m + jnp.log(l)
            o_ref[0, rows, cols] = jnp.where(lo, o2[:ATTN_BLOCK], o2[ATTN_BLOCK:])
            ld_ref[0, rows, cols] = jnp.where(lo, ld2[:ATTN_BLOCK], ld2[ATTN_BLOCK:])


def _prompt_attention(zqkv, g, bias, batch, seq):
    d = DILATIONS[g]
    sub = seq // d
    tq = min(ATTN_TILE, sub)
    nq = sub // tq
    per = tq // ATTN_BLOCK
    zv = zqkv.reshape(batch, sub, d * ALL_QKV)
    base = 3 * g

    def cur(part):
        return pl.BlockSpec((1, tq, ATTN_WIDTH), lambda b, r, n: (b, n, r * N_QKV_TILES + base + part))

    def prev(part):
        return pl.BlockSpec((1, ATTN_BLOCK, ATTN_WIDTH),
                            lambda b, r, n: (b, jnp.maximum(n * per - 1, 0), r * N_QKV_TILES + base + part))

    out_spec = pl.BlockSpec((1, tq, ATTN_WIDTH), lambda b, r, n: (b, n, r))
    o, ld = pl.pallas_call(
        functools.partial(_attn_kernel, tq=tq),
        grid=(batch, d, nq),
        in_specs=[cur(0), cur(1), cur(2), prev(1), prev(2),
                  pl.BlockSpec((2, HEADS // 2, 2 * ATTN_BLOCK, 2 * ATTN_BLOCK), lambda b, r, n: (0, 0, 0, 0))],
        out_specs=[out_spec, out_spec],
        out_shape=[jax.ShapeDtypeStruct((batch, sub, d * ATTN_WIDTH), F32)] * 2,
        scratch_shapes=[pltpu.VMEM((tq + ATTN_BLOCK, ATTN_WIDTH), BF16)] * 2,
        compiler_params=_params("parallel", "parallel", "arbitrary"),
        name=f"prompt_attention_d{d}",
    )(zv, zv, zv, zv, zv, bias)
    return o.reshape(batch * seq, ATTN_WIDTH), ld.reshape(batch * seq, ATTN_WIDTH)


def _alibi_slopes():
    n = N_GROUPS * HEADS
    s = 2.0 ** (-8.0 * jnp.arange(1, n + 1, dtype=F32) / n)
    return s.reshape(N_GROUPS, HEADS)


def _prompt_bias(slopes, d):
    i = jnp.arange(ATTN_BLOCK)[:, None]
    j = jnp.arange(2 * ATTN_BLOCK)[None, :]
    steps = i + ATTN_BLOCK - j
    valid = (steps >= 0) & (steps <= ATTN_BLOCK)
    first = valid & (j >= ATTN_BLOCK)
    bias = -slopes[:, None, None] * (steps * d).astype(F32)
    tables = jnp.stack([jnp.where(first[None], bias, NEG_INF), jnp.where(valid[None], bias, NEG_INF)])
    return tables.reshape(2, HEADS // 2, 2 * ATTN_BLOCK, 2 * ATTN_BLOCK)


def _conv_post(y, cb, lg, lb):
    y = y + cb
    mu = jnp.mean(y, axis=-1, keepdims=True)
    yc = y - mu
    y = yc * lax.rsqrt(jnp.mean(yc * yc, axis=-1, keepdims=True) + NORM_EPS) * lg + lb
    return y * _sigmoid(y)


def _conv_kernel(uc_ref, up_ref, cw_ref, cb_ref, lg_ref, lb_ref, y_ref, buf, *, tt):
    n = pl.program_id(1)
    buf[0:CONV_HALO, :] = jnp.where(n == 0, 0.0, up_ref[0])
    buf[CONV_HALO:, :] = uc_ref[0]
    lead = CONV_HALO - (CONV_WIDTH - 1)
    for c in range(tt // CONV_CHUNK):
        r0 = c * CONV_CHUNK
        acc = jnp.zeros((CONV_CHUNK, CONV_CH), F32)
        for w in range(CONV_WIDTH):
            acc = acc + buf[r0 + lead + w:r0 + lead + w + CONV_CHUNK, :] * cw_ref[w:w + 1, :]
        y_ref[0, r0:r0 + CONV_CHUNK, :] = _conv_post(acc, cb_ref[...], lg_ref[...], lb_ref[...])


def _prompt_conv(u, conv_w, conv_b, ln_g, ln_b, batch, seq):
    tt = min(CONV_TILE, seq)
    u3 = u.reshape(batch, seq, CONV_CH)
    vec = pl.BlockSpec((1, CONV_CH), lambda b, n: (0, 0))
    y = pl.pallas_call(
        functools.partial(_conv_kernel, tt=tt),
        grid=(batch, seq // tt),
        in_specs=[
            pl.BlockSpec((1, tt, CONV_CH), lambda b, n: (b, n, 0)),
            pl.BlockSpec((1, CONV_HALO, CONV_CH),
                         lambda b, n: (b, jnp.maximum(n * (tt // CONV_HALO) - 1, 0), 0)),
            pl.BlockSpec((CONV_WIDTH, CONV_CH), lambda b, n: (0, 0)),
            vec, vec, vec,
        ],
        out_specs=pl.BlockSpec((1, tt, CONV_CH), lambda b, n: (b, n, 0)),
        out_shape=jax.ShapeDtypeStruct((batch, seq, CONV_CH), F32),
        scratch_shapes=[pltpu.VMEM((tt + CONV_HALO, CONV_CH), F32)],
        compiler_params=_params("parallel", "arbitrary"),
        name="prompt_conv",
    )(u3, u3, conv_w, conv_b, ln_g, ln_b)
    return y.reshape(batch * seq, CONV_CH)


def _sample_attn_kernel(z_ref, c0_ref, c1_ref, c2_ref, cb_ref, nb_ref, o_ref, ld_ref, *, t_new):
    caches = (c0_ref, c1_ref, c2_ref)
    row_head = lax.broadcasted_iota(jnp.int32, (HEADS, ATTN_WIDTH), 0)
    lane_head = lax.broadcasted_iota(jnp.int32, (HEADS, ATTN_WIDTH), 1) // HEAD_DIM
    own = row_head == lane_head
    row_w = 2 * ATTN_WIDTH
    for g in range(N_GROUPS):
        d = DILATIONS[g]
        zq = z_ref[0, :, g * QKV_COLS:g * QKV_COLS + ATTN_WIDTH]
        k_new = z_ref[0, :, g * QKV_COLS + ATTN_WIDTH:g * QKV_COLS + 2 * ATTN_WIDTH]
        v_new = z_ref[0, :, g * QKV_COLS + 2 * ATTN_WIDTH:(g + 1) * QKV_COLS]
        for j in range(t_new):
            c = 0 if d == 1 else j * row_w
            kc = caches[g][0, :, c:c + ATTN_WIDTH].astype(BF16)
            vc = caches[g][0, :, c + ATTN_WIDTH:c + row_w].astype(BF16)
            q8 = jnp.where(own, zq[j:j + 1, :] * (HEAD_DIM ** -0.5), 0.0)
            s = lax.dot_general(q8.astype(BF16), kc, (((1,), (1,)), ((), ())),
                                preferred_element_type=F32) + cb_ref[g, j]
            s_new = [jnp.sum(q8 * k_new[jj:jj + 1, :], axis=-1, keepdims=True)
                     + nb_ref[g, j][:, jj:jj + 1] for jj in range(t_new)]
            m = jnp.max(s, axis=-1, keepdims=True)
            for sn in s_new:
                m = jnp.maximum(m, sn)
            p = jnp.exp(s - m)
            l = jnp.sum(p, axis=-1, keepdims=True)
            acc = jnp.dot(p.astype(BF16), vc, preferred_element_type=F32)
            for jj, sn in enumerate(s_new):
                pn = jnp.exp(sn - m)
                l = l + pn
                acc = acc + pn * v_new[jj:jj + 1, :]
            o8 = jnp.where(own, acc / l, 0.0)
            ld8 = jnp.where(own, m + jnp.log(l), 0.0)
            o_ref[0, g, j:j + 1, :] = jnp.sum(o8, axis=0, keepdims=True)
            ld_ref[0, g, j:j + 1, :] = jnp.sum(ld8, axis=0, keepdims=True)


def _sample_attention(zs, caches, cache_bias, new_bias, db, t_new):
    z3 = zs.reshape(db, t_new, ALL_QKV)
    row_w = 2 * ATTN_WIDTH
    views, specs = [], []
    for g in range(N_GROUPS):
        d = DILATIONS[g]
        views.append(caches[g].reshape(db, ATTN_BLOCK, d * row_w))
        width = row_w if d == 1 else t_new * row_w
        specs.append(pl.BlockSpec((1, ATTN_BLOCK, width), lambda b: (b, 0, 0)))
    out_spec = pl.BlockSpec((1, N_GROUPS, t_new, ATTN_WIDTH), lambda b: (b, 0, 0, 0))
    return pl.pallas_call(
        functools.partial(_sample_attn_kernel, t_new=t_new),
        grid=(db,),
        in_specs=[pl.BlockSpec((1, t_new, ALL_QKV), lambda b: (b, 0, 0))] + specs + [
            pl.BlockSpec((N_GROUPS, t_new, HEADS, ATTN_BLOCK), lambda b: (0, 0, 0, 0)),
            pl.BlockSpec((N_GROUPS, t_new, HEADS, LANES), lambda b: (0, 0, 0, 0)),
        ],
        out_specs=[out_spec, out_spec],
        out_shape=[jax.ShapeDtypeStruct((db, N_GROUPS, t_new, ATTN_WIDTH), F32)] * 2,
        compiler_params=_params("parallel"),
        name="sample_attention",
    )(z3, *views, cache_bias, new_bias)


def _sample_bias(slopes, t_new):
    i = jnp.arange(ATTN_BLOCK)[None, :]
    j = jnp.arange(t_new)[:, None]
    jj = jnp.arange(LANES)[None, :]
    cache_tabs, new_tabs = [], []
    for g in range(N_GROUPS):
        d = DILATIONS[g]
        sl = slopes[g][None, :, None]
        if d == 1:
            steps = (ATTN_BLOCK + j - i)[:, None, :]
            cache_tabs.append(jnp.where(steps <= ATTN_BLOCK, -sl * steps.astype(F32), NEG_INF))
            nsteps = (j - jj)[:, None, :]
            new_tabs.append(jnp.where(nsteps >= 0, -sl * nsteps.astype(F32), NEG_INF))
        else:
            assert t_new <= d
            steps = jnp.broadcast_to(ATTN_BLOCK - i, (t_new, ATTN_BLOCK))[:, None, :]
            cache_tabs.append(-sl * (steps * d).astype(F32))
            new_tabs.append(jnp.broadcast_to(jnp.where((j == jj)[:, None, :], 0.0, NEG_INF),
                                             (t_new, HEADS, LANES)))
    return jnp.stack(cache_tabs), jnp.stack(new_tabs)


def _sample_conv_kernel(ctx_ref, cw_ref, cb_ref, lg_ref, lb_ref, y_ref, *, t_new):
    for t in range(t_new):
        acc = jnp.zeros(y_ref.shape[1:], F32)
        for w in range(CONV_WIDTH):
            acc = acc + ctx_ref[t + w] * cw_ref[w:w + 1, :]
        y_ref[t] = _conv_post(acc, cb_ref[...], lg_ref[...], lb_ref[...])


def _sample_conv(ctx_t, conv_w, conv_b, ln_g, ln_b, t_new):
    rows, db, _ = ctx_t.shape
    vec = pl.BlockSpec((1, CONV_CH), lambda i: (0, 0))
    return pl.pallas_call(
        functools.partial(_sample_conv_kernel, t_new=t_new),
        grid=(1,),
        in_specs=[pl.BlockSpec((rows, db, CONV_CH), lambda i: (0, 0, 0)),
                  pl.BlockSpec((CONV_WIDTH, CONV_CH), lambda i: (0, 0)), vec, vec, vec],
        out_specs=pl.BlockSpec((t_new, db, CONV_CH), lambda i: (0, 0, 0)),
        out_shape=jax.ShapeDtypeStruct((t_new, db, CONV_CH), F32),
        compiler_params=_params("arbitrary"),
        name="sample_conv",
    )(ctx_t, conv_w, conv_b, ln_g, ln_b)


def _finish_kernel(x_ref, o0, o1, o2, l0, l1, l2, yc_ref, carry_ref, g1_ref, wg_ref, bg_ref,
                   wab_ref, wcb_ref, wo_ref, g2_ref, wr_ref, br_ref, tri_ref,
                   x2_ref, h2_ref, slot_ref, gate_ref, cnt_ref, carry, *, row0, tm, tile):
    i = pl.program_id(0)

    @pl.when(i == 0)
    def _():
        carry[...] = carry_ref[...]

    @pl.when((row0 + i * tm) % tile == 0)
    def _():
        carry[...] = jnp.zeros_like(carry)

    m = jnp.maximum(jnp.maximum(l0[...], l1[...]), l2[...])
    e0, e1, e2 = jnp.exp(l0[...] - m), jnp.exp(l1[...] - m), jnp.exp(l2[...] - m)
    y_attn = (e0 * o0[...] + e1 * o1[...] + e2 * o2[...]) / (e0 + e1 + e2)

    x = x_ref[...]
    ya = jnp.dot(y_attn.astype(BF16), wab_ref[...], preferred_element_type=F32)
    yc = jnp.dot(yc_ref[...].astype(BF16), wcb_ref[...], preferred_element_type=F32)
    h = _rms(x, g1_ref[...]).astype(BF16)
    gates = jnp.dot(h, wg_ref[...], preferred_element_type=F32) + bg_ref[...]
    mixed = _sigmoid(gates[:, :D_MODEL]) * ya + _sigmoid(gates[:, D_MODEL:]) * yc
    x2 = x + jnp.dot(mixed.astype(BF16), wo_ref[...], preferred_element_type=F32)
    x2_ref[...] = x2
    h2 = _rms(x2, g2_ref[...])
    h2_ref[...] = h2.astype(BF16)

    logits = jnp.dot(h2, wr_ref[...], preferred_element_type=F32,
                     precision=lax.Precision.HIGHEST) + br_ref[...]
    lane = lax.broadcasted_iota(jnp.int32, logits.shape, 1).astype(F32)
    work = jnp.where(lane < N_EXPERTS, logits, -jnp.inf)
    sel = jnp.zeros(logits.shape, jnp.bool_)
    top = None
    den = jnp.zeros((tm, 1), F32)
    for _ in range(TOP_K):
        mx = jnp.max(work, axis=-1, keepdims=True)
        idx = jnp.min(jnp.where(work == mx, lane, float(LANES)), axis=-1, keepdims=True)
        pick = lane == idx
        sel = sel | pick
        top = mx if top is None else top
        den = den + jnp.exp(mx - top)
        work = jnp.where(pick, -jnp.inf, work)
    gate_ref[...] = jnp.where(sel, jnp.exp(logits - top) / den, 0.0)

    self = jnp.where(sel, 1.0, 0.0)
    before = jnp.dot(tri_ref[...], self.astype(BF16), preferred_element_type=F32) + carry[...]
    slot_ref[...] = jnp.where(sel, before, -1.0)
    carry[...] = carry[...] + jnp.sum(self, axis=0, keepdims=True)
    cnt_ref[0] = jnp.broadcast_to(carry[...], cnt_ref.shape[1:])


def _finish(x2d, groups, y_conv, carry_in, row0, w, tile):
    n = x2d.shape[0]
    tm = min(FIN_TILE, n)
    steps = n // tm
    row = lambda width: pl.BlockSpec((tm, width), lambda i: (i, 0))
    full = lambda a: pl.BlockSpec(a.shape, lambda i: (0,) * a.ndim)
    weights = (w["g1"], w["w_gate"], w["b_gate"], w["w_ab"], w["w_cb"], w["w_out"], w["g2"],
               w["w_router"], w["b_router"], w["tri"])
    outs = pl.pallas_call(
        functools.partial(_finish_kernel, row0=row0, tm=tm, tile=tile),
        grid=(steps,),
        in_specs=[row(D_MODEL)] + [row(ATTN_WIDTH)] * 7 + [full(carry_in)] + [full(a) for a in weights],
        out_specs=[row(D_MODEL), row(D_MODEL), row(LANES), row(LANES),
                   pl.BlockSpec((1, 8, LANES), lambda i: (i, 0, 0))],
        out_shape=[jax.ShapeDtypeStruct((n, D_MODEL), F32), jax.ShapeDtypeStruct((n, D_MODEL), BF16),
                   jax.ShapeDtypeStruct((n, LANES), F32), jax.ShapeDtypeStruct((n, LANES), F32),
                   jax.ShapeDtypeStruct((steps, 8, LANES), F32)],
        scratch_shapes=[pltpu.VMEM((1, LANES), F32)],
        compiler_params=_params("arbitrary"),
        name="finish",
    )(x2d, *[o for o, _ in groups], *[ld for _, ld in groups], y_conv, carry_in, *weights)
    return outs


def _moe_kernel(cnt_ref, h_ref, x2_ref, slot_ref, gate_ref, wgu_ref, bgu_ref, wdn_ref, bdn_ref, o_ref,
                *, tile, chunk):
    t = pl.program_id(0)
    e = pl.program_id(1)

    @pl.when(e == 0)
    def _():
        o_ref[...] = x2_ref[...]

    count = cnt_ref[t * N_EXPERTS + e]
    slots = slot_ref[0, pl.ds(e, 1), :]
    gates = gate_ref[0, pl.ds(e, 1), :]

    def body(c, carry):
        want = (lax.broadcasted_iota(jnp.int32, (chunk, tile), 0) + c * chunk).astype(F32)
        hit = slots == want
        onehot = jnp.where(hit, 1.0, 0.0).astype(BF16)
        row_gate = jnp.sum(jnp.where(hit, gates, 0.0), axis=-1, keepdims=True)
        xg = jnp.dot(onehot, h_ref[...], preferred_element_type=F32).astype(BF16)
        gu = jnp.dot(xg, wgu_ref[0], preferred_element_type=F32) + bgu_ref[0]
        glu = jnp.minimum(gu[:, :D_FF], SWIGLU_LIMIT)
        lin = jnp.clip(gu[:, D_FF:], -SWIGLU_LIMIT, SWIGLU_LIMIT)
        act = glu * _sigmoid(SWIGLU_ALPHA * glu) * (lin + 1.0)
        yb = (jnp.dot(act.astype(BF16), wdn_ref[0], preferred_element_type=F32) + bdn_ref[0]) * row_gate
        o_ref[...] += lax.dot_general(onehot, yb.astype(BF16), (((0,), (0,)), ((), ())),
                                      preferred_element_type=F32)
        return carry

    lax.fori_loop(0, (count + chunk - 1) // chunk, body, 0)


def _experts(counts, h2, x2, slot_t, gate_t, w_gu, b_gu, w_dn, b_dn, tile):
    n = h2.shape[0]
    tiles = n // tile
    chunk = min(MOE_CHUNK, tile)
    grid_spec = pltpu.PrefetchScalarGridSpec(
        num_scalar_prefetch=1,
        grid=(tiles, N_EXPERTS),
        in_specs=[
            pl.BlockSpec((tile, D_MODEL), lambda t, e, c: (t, 0)),
            pl.BlockSpec((tile, D_MODEL), lambda t, e, c: (t, 0), pipeline_mode=pl.Buffered(1)),
            pl.BlockSpec((1, N_EXPERTS, tile), lambda t, e, c: (t, 0, 0)),
            pl.BlockSpec((1, N_EXPERTS, tile), lambda t, e, c: (t, 0, 0)),
            pl.BlockSpec((1, D_MODEL, 2 * D_FF), lambda t, e, c: (e, 0, 0)),
            pl.BlockSpec((1, 1, 2 * D_FF), lambda t, e, c: (e, 0, 0)),
            pl.BlockSpec((1, D_FF, D_MODEL), lambda t, e, c: (e, 0, 0)),
            pl.BlockSpec((1, 1, D_MODEL), lambda t, e, c: (e, 0, 0)),
        ],
        out_specs=pl.BlockSpec((tile, D_MODEL), lambda t, e, c: (t, 0)),
    )
    return pl.pallas_call(
        functools.partial(_moe_kernel, tile=tile, chunk=chunk),
        grid_spec=grid_spec,
        out_shape=jax.ShapeDtypeStruct((n, D_MODEL), F32),
        compiler_params=_params("parallel", "arbitrary"),
        name="experts",
    )(counts, h2, x2, slot_t, gate_t, w_gu, b_gu, w_dn, b_dn)


def _moe_tile(n):
    for tile in (MOE_TILE, 1024, 512, 256):
        if n % tile == 0:
            return tile
    raise ValueError(f"token count {n} is not a multiple of {FIN_TILE}")


def _layer(xp, xs, kv_caches, conv_state, p):
    batch, seq, _ = xp.shape
    db, t_new, _ = xs.shape
    n_p, n_s = batch * seq, db * t_new
    tile = _moe_tile(n_p + n_s)
    assert seq % (DILATIONS[-1] * ATTN_BLOCK) == 0 and seq >= WINDOWS[-1]
    assert n_p % FIN_TILE == 0 and (n_s % FIN_TILE == 0 or n_s < FIN_TILE) and tile % min(FIN_TILE, n_s) == 0
    for g in range(N_GROUPS):
        assert kv_caches[g].shape[1] == WINDOWS[g]

    w_in = p["w_in"]
    w_qkv_glu = w_in[:, :ALL_QKV + 2 * CONV_CH].astype(BF16)
    qk_gain = jnp.zeros((8, ATTN_WIDTH), F32)
    qk_gain = qk_gain.at[0].set(jnp.tile(p["q_norm_gain"], HEADS)).at[1].set(jnp.tile(p["k_norm_gain"], HEADS))
    head_of = jnp.arange(MXU_DIM) // HEAD_DIM
    bd = (head_of[:, None] == head_of[None, :]).astype(BF16)
    g1 = p["norm1_gain"][None, :]
    fin_tile = min(FIN_TILE, n_s)
    tri = (jnp.arange(fin_tile)[:, None] > jnp.arange(fin_tile)[None, :]).astype(BF16)
    wfin = dict(
        g1=g1, w_gate=w_in[:, ALL_QKV + 2 * CONV_CH:].astype(BF16), b_gate=p["b_gate"].reshape(1, 2 * D_MODEL),
        w_ab=p["w_attn_branch"].astype(BF16), w_cb=p["w_conv_branch"].astype(BF16),
        w_out=p["w_out"].astype(BF16), g2=p["norm2_gain"][None, :],
        w_router=jnp.pad(p["w_router"], ((0, 0), (0, LANES - N_EXPERTS))),
        b_router=jnp.pad(p["b_router"], (0, LANES - N_EXPERTS))[None, :], tri=tri)
    cvec = lambda a: a[None, :]
    conv_args = (p["conv_w"], cvec(p["conv_b"]), cvec(p["conv_ln_gain"]), cvec(p["conv_ln_bias"]))
    slopes = _alibi_slopes()

    xp2 = xp.reshape(n_p, D_MODEL)
    zp, up = _in_projection(xp2, g1, w_qkv_glu, qk_gain, bd)
    groups_p = [_prompt_attention(zp, g, _prompt_bias(slopes[g], DILATIONS[g]), batch, seq)
                for g in range(N_GROUPS)]
    yconv_p = _prompt_conv(up, *conv_args, batch, seq)
    zp4 = zp.reshape(batch, seq, N_GROUPS, 3, HEADS, HEAD_DIM)
    kv_p = [zp4[:, seq - WINDOWS[g]:, g, 1:3][None] for g in range(N_GROUPS)]
    conv_p = up.reshape(batch, seq, CONV_CH)[:, seq - (CONV_WIDTH - 1):][None]

    xs2 = xs.reshape(n_s, D_MODEL)
    zs, us = _in_projection(xs2, g1, w_qkv_glu, qk_gain, bd)
    cache_bias, new_bias = _sample_bias(slopes, t_new)
    o_s, ld_s = _sample_attention(zs, kv_caches, cache_bias, new_bias, db, t_new)
    groups_s = [(o_s[:, g].reshape(n_s, ATTN_WIDTH), ld_s[:, g].reshape(n_s, ATTN_WIDTH))
                for g in range(N_GROUPS)]
    u_ctx = jnp.concatenate([conv_state, us.reshape(db, t_new, CONV_CH)], axis=1)
    yconv_s = _sample_conv(jnp.swapaxes(u_ctx, 0, 1), *conv_args, t_new)
    yconv_s = jnp.swapaxes(yconv_s, 0, 1).reshape(n_s, CONV_CH)
    zs4 = zs.reshape(db, t_new, N_GROUPS, 3, HEADS, HEAD_DIM)
    kv_s = [jnp.concatenate([kv_caches[g][:, t_new:], zs4[:, :, g, 1:3]], axis=1)[None]
            for g in range(N_GROUPS)]
    conv_s = u_ctx[:, t_new:][None]

    zero_carry = jnp.zeros((1, LANES), F32)
    x2p, h2p, slot_p, gate_p, cnt_p = _finish(xp2, groups_p, yconv_p, zero_carry, 0, wfin, tile)
    x2s, h2s, slot_s, gate_s, cnt_s = _finish(xs2, groups_s, yconv_s, cnt_p[-1, :1], n_p, wfin, tile)
    n = n_p + n_s
    tiles = n // tile
    x2 = jnp.concatenate([x2p, x2s])
    h2 = jnp.concatenate([h2p, h2s])

    def tile_major(a):
        return jnp.swapaxes(a.reshape(tiles, tile, LANES), 1, 2)[:, :N_EXPERTS]

    slot_t = tile_major(jnp.concatenate([slot_p, slot_s]))
    gate_t = tile_major(jnp.concatenate([gate_p, gate_s]))
    cnt = jnp.concatenate([cnt_p[:, 0], cnt_s[:, 0]])
    per_tile = tile // fin_tile
    counts = cnt[per_tile - 1::per_tile, :N_EXPERTS].astype(jnp.int32).reshape(-1)

    y = _experts(counts, h2, x2, slot_t, gate_t, p["w_gate_up"].astype(BF16), p["b_gate_up"][:, None, :],
                 p["w_down"].astype(BF16), p["b_down"][:, None, :], tile)
    return y[:n_p].reshape(batch, seq, D_MODEL), y[n_p:].reshape(db, t_new, D_MODEL), kv_p, conv_p, kv_s, conv_s


def kernel(x_prompt, x_sample, cache_kv_w128, cache_kv_w512, cache_kv_w2048, state_conv, norm1_gain, w_in,
           q_norm_gain, k_norm_gain, b_gate, conv_w, conv_b, conv_ln_gain, conv_ln_bias, w_attn_branch,
           w_conv_branch, w_out, norm2_gain, w_router, b_router, w_gate_up, b_gate_up, w_down, b_down):
    depth = w_in.shape[0]
    assert depth == 1, "one layer per step"
    params = dict(norm1_gain=norm1_gain, w_in=w_in, q_norm_gain=q_norm_gain, k_norm_gain=k_norm_gain,
                  b_gate=b_gate, conv_w=conv_w, conv_b=conv_b, conv_ln_gain=conv_ln_gain,
                  conv_ln_bias=conv_ln_bias, w_attn_branch=w_attn_branch, w_conv_branch=w_conv_branch,
                  w_out=w_out, norm2_gain=norm2_gain, w_router=w_router, b_router=b_router,
                  w_gate_up=w_gate_up, b_gate_up=b_gate_up, w_down=w_down, b_down=b_down)
    p = {k: v[0] for k, v in params.items()}
    caches = (cache_kv_w128[0], cache_kv_w512[0], cache_kv_w2048[0])
    y_p, y_s, kv_p, conv_p, kv_s, conv_s = _layer(x_prompt, x_sample, caches, state_conv[0], p)
    return (y_p, y_s, kv_p[0], kv_p[1], kv_p[2], conv_p, kv_s[0], kv_s[1], kv_s[2], conv_s)
```

```python
import functools

import jax
import jax.numpy as jnp
from jax import lax
from jax.experimental import pallas as pl
from jax.experimental.pallas import tpu as pltpu

F32 = jnp.float32
BF16 = jnp.bfloat16

D_MODEL = 1024
HEAD_DIM = 64
HEADS = 8
ATTN_WIDTH = HEADS * HEAD_DIM
N_GROUPS = 3
WINDOWS = (128, 512, 2048)
DILATIONS = (1, 4, 16)
ATTN_BLOCK = 128
QKV_COLS = 3 * ATTN_WIDTH
ALL_QKV = N_GROUPS * QKV_COLS
CONV_CH = 512
CONV_WIDTH = 31
CONV_HALO = 32
N_EXPERTS = 32
TOP_K = 4
D_FF = 1024
SWIGLU_ALPHA = 1.702
SWIGLU_LIMIT = 7.0
NORM_EPS = 1e-6
NEG_INF = -1e30
FAR = 1e30

LANES = 128
LANE_TILES = ATTN_WIDTH // LANES
MXU_DIM = 256
VMEM_LIMIT = 56 * 1024 * 1024

IN_TILE = 512
ATTN_TILE = 512
CONV_TILE = 256
CONV_CHUNK = 64
FIN_TILE = 256
MOE_TILE = 1536
MOE_CHUNK = 256


def _sigmoid(x):
    return 1.0 / (1.0 + jnp.exp(-x))


def _rms(x, gain):
    ms = jnp.mean(x * x, axis=-1, keepdims=True)
    return x * lax.rsqrt(ms + NORM_EPS) * gain


def _params(*sem):
    return pltpu.CompilerParams(dimension_semantics=sem, vmem_limit_bytes=VMEM_LIMIT)


def _resident(shape):
    return pl.BlockSpec(shape, lambda *_: (0,) * len(shape), pipeline_mode=pl.Buffered(1))


def _alibi_slopes():
    n = N_GROUPS * HEADS
    s = 2.0 ** (-8.0 * jnp.arange(1, n + 1, dtype=F32) / n)
    return s.reshape(N_GROUPS, HEADS)


def _inproj_kernel(x_ref, g1_ref, w_ref, qk_ref, bd_ref, z0_ref, z1_ref, z2_ref, u_ref, scr, *, tm, dils):
    h = _rms(x_ref[...], g1_ref[...]).astype(BF16)
    bd = bd_ref[...]
    outs = (z0_ref, z1_ref, z2_ref)
    for g in range(N_GROUPS):
        d = dils[g]
        for part in range(3):
            col = g * QKV_COLS + part * ATTN_WIDTH
            z = jnp.dot(h, w_ref[:, col:col + ATTN_WIDTH], preferred_element_type=F32)
            if part < 2:
                zz = (z * z).astype(BF16)
                ss = jnp.concatenate(
                    [jnp.dot(zz[:, :MXU_DIM], bd, preferred_element_type=F32),
                     jnp.dot(zz[:, MXU_DIM:], bd, preferred_element_type=F32)], axis=1)
                z = z * lax.rsqrt(ss * (1.0 / HEAD_DIM) + NORM_EPS) * qk_ref[part:part + 1, :]
            cols = slice(part * ATTN_WIDTH, (part + 1) * ATTN_WIDTH)
            if d == 1:
                outs[g][0, 0, :, cols] = z
            else:
                for c in range(LANE_TILES):
                    scr[c] = z[:, c * LANES:(c + 1) * LANES]
                for r in range(d):
                    for c in range(LANE_TILES):
                        lanes = slice(part * ATTN_WIDTH + c * LANES, part * ATTN_WIDTH + (c + 1) * LANES)
                        outs[g][0, r, :, lanes] = scr.at[c][pl.ds(r, tm // d, stride=d), :]
    a = jnp.dot(h, w_ref[:, ALL_QKV:ALL_QKV + CONV_CH], preferred_element_type=F32)
    gate = jnp.dot(h, w_ref[:, ALL_QKV + CONV_CH:], preferred_element_type=F32)
    u_ref[...] = a * _sigmoid(gate)


def _in_projection(x2d, g1, w_bf, qk_gain, bd, batch, seq, dils):
    n = x2d.shape[0]
    tm = min(IN_TILE, seq)
    per_b = seq // tm
    assert all(tm % (8 * d) == 0 for d in dils)

    def zspec(d):
        return pl.BlockSpec((1, d, tm // d, QKV_COLS), lambda i: (i // per_b, 0, i % per_b, 0))

    return pl.pallas_call(
        functools.partial(_inproj_kernel, tm=tm, dils=dils),
        grid=(n // tm,),
        in_specs=[
            pl.BlockSpec((tm, D_MODEL), lambda i: (i, 0)),
            _resident((1, D_MODEL)),
            _resident(w_bf.shape),
            _resident((8, ATTN_WIDTH)),
            _resident((MXU_DIM, MXU_DIM)),
        ],
        out_specs=[zspec(d) for d in dils] + [pl.BlockSpec((tm, CONV_CH), lambda i: (i, 0))],
        out_shape=[jax.ShapeDtypeStruct((batch, d, seq // d, QKV_COLS), F32) for d in dils]
        + [jax.ShapeDtypeStruct((n, CONV_CH), F32)],
        scratch_shapes=[pltpu.VMEM((LANE_TILES, tm, LANES), F32)],
        compiler_params=_params("parallel"),
        name="in_projection",
    )(x2d, g1, w_bf, qk_gain, bd)


def _attn_kernel(q_ref, kc_ref, vc_ref, kp_ref, vp_ref, bias_ref, o_ref, ld_ref, kf, vf, *, tq):
    n = pl.program_id(2)
    kf[0:ATTN_BLOCK, :] = kp_ref[0, 0].astype(BF16)
    kf[ATTN_BLOCK:, :] = kc_ref[0, 0].astype(BF16)
    vf[0:ATTN_BLOCK, :] = vp_ref[0, 0].astype(BF16)
    vf[ATTN_BLOCK:, :] = vc_ref[0, 0].astype(BF16)
    lo = lax.broadcasted_iota(jnp.int32, (ATTN_BLOCK, LANES), 1) < HEAD_DIM
    for i in range(tq // ATTN_BLOCK):
        rows = slice(i * ATTN_BLOCK, (i + 1) * ATTN_BLOCK)
        krows = slice(i * ATTN_BLOCK, (i + 2) * ATTN_BLOCK)
        table = jnp.where(n == 0, 0, 1) if i == 0 else 1
        for hp in range(HEADS // 2):
            cols = slice(hp * LANES, (hp + 1) * LANES)
            qp = q_ref[0, 0, rows, cols] * (HEAD_DIM ** -0.5)
            qq = jnp.concatenate([jnp.where(lo, qp, 0.0), jnp.where(lo, 0.0, qp)], axis=0).astype(BF16)
            s = lax.dot_general(qq, kf[krows, cols], (((1,), (1,)), ((), ())),
                                preferred_element_type=F32)
            s = s + bias_ref[table, hp]
            m = jnp.max(s, axis=-1, keepdims=True)
            p = jnp.exp(s - m)
            l = jnp.sum(p, axis=-1, keepdims=True)
            o2 = jnp.dot(p.astype(BF16), vf[krows, cols], preferred_element_type=F32) / l
            ld2 = m + jnp.log(l)
            o_ref[0, 0, rows, cols] = jnp.where(lo, o2[:ATTN_BLOCK], o2[ATTN_BLOCK:])
            ld_ref[0, 0, rows, cols] = jnp.where(lo, ld2[:ATTN_BLOCK], ld2[ATTN_BLOCK:])


def _prompt_attention(zg, bias):
    batch, d, sub, _ = zg.shape
    tq = min(ATTN_TILE, sub)
    per = tq // ATTN_BLOCK

    def cur(part):
        return pl.BlockSpec((1, 1, tq, ATTN_WIDTH), lambda b, r, n: (b, r, n, part))

    def prev(part):
        return pl.BlockSpec((1, 1, ATTN_BLOCK, ATTN_WIDTH),
                            lambda b, r, n: (b, r, jnp.maximum(n * per - 1, 0), part))

    return pl.pallas_call(
        functools.partial(_attn_kernel, tq=tq),
        grid=(batch, d, sub // tq),
        in_specs=[cur(0), cur(1), cur(2), prev(1), prev(2), _resident(bias.shape)],
        out_specs=[cur(0), cur(0)],
        out_shape=[jax.ShapeDtypeStruct((batch, d, sub, ATTN_WIDTH), F32)] * 2,
        scratch_shapes=[pltpu.VMEM((tq + ATTN_BLOCK, ATTN_WIDTH), BF16)] * 2,
        compiler_params=_params("parallel", "parallel", "arbitrary"),
        name=f"prompt_attention_d{d}",
    )(zg, zg, zg, zg, zg, bias)


def _prompt_bias(slopes, d):
    i = jnp.arange(ATTN_BLOCK)[:, None]
    j = jnp.arange(2 * ATTN_BLOCK)[None, :]
    steps = i + ATTN_BLOCK - j
    valid = (steps >= 0) & (steps <= ATTN_BLOCK)
    first = valid & (j >= ATTN_BLOCK)
    bias = -slopes[:, None, None] * (steps * d).astype(F32)
    tables = jnp.stack([jnp.where(first[None], bias, NEG_INF), jnp.where(valid[None], bias, NEG_INF)])
    return tables.reshape(2, HEADS // 2, 2 * ATTN_BLOCK, 2 * ATTN_BLOCK)


def _conv_post(y, cb, lg, lb):
    y = y + cb
    mu = jnp.mean(y, axis=-1, keepdims=True)
    yc = y - mu
    y = yc * lax.rsqrt(jnp.mean(yc * yc, axis=-1, keepdims=True) + NORM_EPS) * lg + lb
    return y * _sigmoid(y)


def _conv_kernel(uc_ref, up_ref, cw_ref, cb_ref, lg_ref, lb_ref, y_ref, buf, *, tt):
    n = pl.program_id(1)
    buf[0:CONV_HALO, :] = jnp.where(n == 0, 0.0, up_ref[0])
    buf[CONV_HALO:, :] = uc_ref[0]
    lead = CONV_HALO - (CONV_WIDTH - 1)
    for c in range(tt // CONV_CHUNK):
        r0 = c * CONV_CHUNK
        acc = jnp.zeros((CONV_CHUNK, CONV_CH), F32)
        for w in range(CONV_WIDTH):
            acc = acc + buf[r0 + lead + w:r0 + lead + w + CONV_CHUNK, :] * cw_ref[w:w + 1, :]
        y_ref[0, r0:r0 + CONV_CHUNK, :] = _conv_post(acc, cb_ref[...], lg_ref[...], lb_ref[...])


def _prompt_conv(u, conv_w, conv_b, ln_g, ln_b, batch, seq):
    tt = min(CONV_TILE, seq)
    u3 = u.reshape(batch, seq, CONV_CH)
    y = pl.pallas_call(
        functools.partial(_conv_kernel, tt=tt),
        grid=(batch, seq // tt),
        in_specs=[
            pl.BlockSpec((1, tt, CONV_CH), lambda b, n: (b, n, 0)),
            pl.BlockSpec((1, CONV_HALO, CONV_CH),
                         lambda b, n: (b, jnp.maximum(n * (tt // CONV_HALO) - 1, 0), 0)),
            _resident((CONV_WIDTH, CONV_CH)), _resident((1, CONV_CH)), _resident((1, CONV_CH)),
            _resident((1, CONV_CH)),
        ],
        out_specs=pl.BlockSpec((1, tt, CONV_CH), lambda b, n: (b, n, 0)),
        out_shape=jax.ShapeDtypeStruct((batch, seq, CONV_CH), F32),
        scratch_shapes=[pltpu.VMEM((tt + CONV_HALO, CONV_CH), F32)],
        compiler_params=_params("parallel", "arbitrary"),
        name="prompt_conv",
    )(u3, u3, conv_w, conv_b, ln_g, ln_b)
    return y.reshape(batch * seq, CONV_CH)


def _sample_conv_kernel(ctx_ref, cw_ref, cb_ref, lg_ref, lb_ref, y_ref, *, t_new):
    for t in range(t_new):
        acc = jnp.zeros(y_ref.shape[1:], F32)
        for w in range(CONV_WIDTH):
            acc = acc + ctx_ref[t + w] * cw_ref[w:w + 1, :]
        y_ref[t] = _conv_post(acc, cb_ref[...], lg_ref[...], lb_ref[...])


def _sample_conv(ctx_t, conv_w, conv_b, ln_g, ln_b, t_new):
    rows, db, _ = ctx_t.shape
    return pl.pallas_call(
        functools.partial(_sample_conv_kernel, t_new=t_new),
        grid=(1,),
        in_specs=[_resident((rows, db, CONV_CH)), _resident((CONV_WIDTH, CONV_CH)), _resident((1, CONV_CH)),
                  _resident((1, CONV_CH)), _resident((1, CONV_CH))],
        out_specs=pl.BlockSpec((t_new, db, CONV_CH), lambda i: (0, 0, 0)),
        out_shape=jax.ShapeDtypeStruct((t_new, db, CONV_CH), F32),
        compiler_params=_params("arbitrary"),
        name="sample_conv",
    )(ctx_t, conv_w, conv_b, ln_g, ln_b)


def _sample_cache_kernel(c_ref, new_ref, z_ref, bold_ref, bnew_ref, out_ref, o_ref, ld_ref, kb, vb,
                         *, length, t_new, g):
    lane = lax.broadcasted_iota(jnp.int32, (HEAD_DIM, LANES), 1)
    is_new = lane >= LANES - t_new
    for kv, dst in ((0, kb), (1, vb)):
        for h in range(HEADS):
            rows = slice(h * HEAD_DIM, (h + 1) * HEAD_DIM)
            old = c_ref[0, kv, rows, :]
            dst[rows, :] = old.astype(BF16)
            moved = pltpu.roll(old, length - t_new, 1)
            if length > LANES:
                out_ref[0, kv, rows, :length - LANES] = moved[:, :length - LANES]
            out_ref[0, kv, rows, length - LANES:] = jnp.where(is_new, new_ref[0, kv, rows, :],
                                                              moved[:, length - LANES:])
    zq = z_ref[0, :, g * QKV_COLS:g * QKV_COLS + ATTN_WIDTH] * (HEAD_DIM ** -0.5)
    row_head = lax.broadcasted_iota(jnp.int32, (HEADS, ATTN_WIDTH), 0)
    lane_head = lax.broadcasted_iota(jnp.int32, (HEADS, ATTN_WIDTH), 1) // HEAD_DIM
    own = row_head == lane_head
    qbd = jnp.concatenate([jnp.where(own, zq[j:j + 1, :], 0.0) for j in range(t_new)], axis=0).astype(BF16)
    k_new = new_ref[0, 0].astype(BF16)
    v_new = new_ref[0, 1].astype(BF16)
    s_old = jnp.dot(qbd, kb[...], preferred_element_type=F32) + bold_ref[...]
    s_new = jnp.dot(qbd, k_new, preferred_element_type=F32) + bnew_ref[...]
    m = jnp.maximum(jnp.max(s_old, axis=-1, keepdims=True), jnp.max(s_new, axis=-1, keepdims=True))
    p_old = jnp.exp(s_old - m)
    p_new = jnp.exp(s_new - m)
    l = jnp.sum(p_old, axis=-1, keepdims=True) + jnp.sum(p_new, axis=-1, keepdims=True)
    contract_lanes = (((1,), (1,)), ((), ()))
    acc = (lax.dot_general(p_old.astype(BF16), vb[...], contract_lanes, preferred_element_type=F32)
           + lax.dot_general(p_new.astype(BF16), v_new, contract_lanes, preferred_element_type=F32))
    o_rows = acc / l
    ld_rows = m + jnp.log(l)
    for j in range(t_new):
        rows = slice(j * HEADS, (j + 1) * HEADS)
        o_ref[0, j:j + 1, :] = jnp.sum(jnp.where(own, o_rows[rows], 0.0), axis=0, keepdims=True)
        ld_ref[0, j:j + 1, :] = jnp.sum(jnp.where(own, ld_rows[rows], 0.0), axis=0, keepdims=True)


def _sample_bias(slopes_g, length, d, t_new):
    j = jnp.arange(t_new)[:, None]
    back = length + j - jnp.arange(length)[None, :]
    ok = (back % d == 0) & (back <= ATTN_BLOCK * d)
    d_old = jnp.where(ok, back.astype(F32), FAR)
    jj = jnp.arange(LANES)[None, :] - (LANES - t_new)
    back = j - jj
    ok = (jj >= 0) & (back >= 0) & (back % d == 0)
    d_new = jnp.where(ok, back.astype(F32), FAR)
    expand = lambda dist: (-slopes_g[None, :, None] * dist[:, None, :]).reshape(t_new * HEADS, dist.shape[-1])
    return expand(d_old), expand(d_new)


def _sample_cache_step(cache, zs3, g, slopes_g, t_new):
    db, length = cache.shape[:2]
    d = DILATIONS[g]
    cache_t = jnp.transpose(cache, (0, 2, 3, 4, 1)).reshape(db, 2, ATTN_WIDTH, length)
    kv_new = zs3[:, :, g * QKV_COLS + ATTN_WIDTH:(g + 1) * QKV_COLS].reshape(db, t_new, 2, ATTN_WIDTH)
    new_t = jnp.pad(jnp.transpose(kv_new, (0, 2, 3, 1)), ((0, 0),) * 3 + ((LANES - t_new, 0),))
    b_old, b_new = _sample_bias(slopes_g, length, d, t_new)
    blk = lambda last: pl.BlockSpec((1, 2, ATTN_WIDTH, last), lambda b: (b, 0, 0, 0))
    row_spec = pl.BlockSpec((1, t_new, ATTN_WIDTH), lambda b: (b, 0, 0))
    out_t, o, ld = pl.pallas_call(
        functools.partial(_sample_cache_kernel, length=length, t_new=t_new, g=g),
        grid=(db,),
        in_specs=[blk(length), blk(LANES), pl.BlockSpec((1, t_new, ALL_QKV), lambda b: (b, 0, 0)),
                  _resident(b_old.shape), _resident(b_new.shape)],
        out_specs=[blk(length), row_spec, row_spec],
        out_shape=[jax.ShapeDtypeStruct((db, 2, ATTN_WIDTH, length), F32),
                   jax.ShapeDtypeStruct((db, t_new, ATTN_WIDTH), F32),
                   jax.ShapeDtypeStruct((db, t_new, ATTN_WIDTH), F32)],
        scratch_shapes=[pltpu.VMEM((ATTN_WIDTH, length), BF16)] * 2,
        compiler_params=_params("parallel"),
        name=f"sample_cache_d{d}",
    )(cache_t, new_t, zs3, b_old, b_new)
    new_cache = jnp.transpose(out_t.reshape(db, 2, HEADS, HEAD_DIM, length), (0, 4, 1, 2, 3))
    return new_cache, o.reshape(db * t_new, ATTN_WIDTH), ld.reshape(db * t_new, ATTN_WIDTH)


def _finish_kernel(*refs, tm, p_steps, tile):
    (xp, op0, op1, op2, lp0, lp1, lp2, ycp, xs, os0, os1, os2, ls0, ls1, ls2, ycs,
     g1_ref, wg_ref, bg_ref, wab_ref, wcb_ref, wo_ref, g2_ref, wr_ref, br_ref, tri_ref,
     x2_ref, h2_ref, slot_ref, gate_ref, cnt_ref,
     xb, ob0, ob1, ob2, lb0, lb1, lb2, ycb, carry) = refs
    i = pl.program_id(0)
    bufs = (xb, ob0, ob1, ob2, lb0, lb1, lb2, ycb)

    @pl.when(i < p_steps)
    def _():
        xb[...] = xp[...]
        ycb[...] = ycp[...]
        for g, (src_o, src_l) in enumerate(((op0, lp0), (op1, lp1), (op2, lp2))):
            d = DILATIONS[g]
            for src, dst in ((src_o, bufs[1 + g]), (src_l, bufs[4 + g])):
                for c in range(LANE_TILES):
                    lanes = slice(c * LANES, (c + 1) * LANES)
                    for r in range(d):
                        dst.at[c][pl.ds(r, tm // d, stride=d), :] = src[0, r, :, lanes]

    @pl.when(i >= p_steps)
    def _():
        xb[...] = xs[...]
        ycb[...] = ycs[...]
        for src, dst in zip((os0, os1, os2, ls0, ls1, ls2), bufs[1:7]):
            for c in range(LANE_TILES):
                dst[c] = src[:, c * LANES:(c + 1) * LANES]

    @pl.when((i * tm) % tile == 0)
    def _():
        carry[...] = jnp.zeros_like(carry)

    wide = lambda buf: jnp.concatenate([buf[c] for c in range(LANE_TILES)], axis=1)
    l0, l1, l2 = wide(lb0), wide(lb1), wide(lb2)
    m = jnp.maximum(jnp.maximum(l0, l1), l2)
    e0, e1, e2 = jnp.exp(l0 - m), jnp.exp(l1 - m), jnp.exp(l2 - m)
    y_attn = (e0 * wide(ob0) + e1 * wide(ob1) + e2 * wide(ob2)) / (e0 + e1 + e2)

    x = xb[...]
    ya = jnp.dot(y_attn.astype(BF16), wab_ref[...], preferred_element_type=F32)
    yc = jnp.dot(ycb[...].astype(BF16), wcb_ref[...], preferred_element_type=F32)
    h = _rms(x, g1_ref[...]).astype(BF16)
    gates = jnp.dot(h, wg_ref[...], preferred_element_type=F32) + bg_ref[...]
    mixed = _sigmoid(gates[:, :D_MODEL]) * ya + _sigmoid(gates[:, D_MODEL:]) * yc
    x2 = x + jnp.dot(mixed.astype(BF16), wo_ref[...], preferred_element_type=F32)
    x2_ref[...] = x2
    h2 = _rms(x2, g2_ref[...])
    h2_ref[...] = h2.astype(BF16)

    logits = jnp.dot(h2, wr_ref[...], preferred_element_type=F32,
                     precision=lax.Precision.HIGHEST) + br_ref[...]
    lane = lax.broadcasted_iota(jnp.int32, logits.shape, 1).astype(F32)
    work = jnp.where(lane < N_EXPERTS, logits, -jnp.inf)
    sel = jnp.zeros(logits.shape, jnp.bool_)
    top = None
    den = jnp.zeros((tm, 1), F32)
    for _ in range(TOP_K):
        mx = jnp.max(work, axis=-1, keepdims=True)
        idx = jnp.min(jnp.where(work == mx, lane, float(LANES)), axis=-1, keepdims=True)
        pick = lane == idx
        sel = sel | pick
        top = mx if top is None else top
        den = den + jnp.exp(mx - top)
        work = jnp.where(pick, -jnp.inf, work)
    gate_ref[...] = jnp.where(sel, jnp.exp(logits - top) / den, 0.0)

    self = jnp.where(sel, 1.0, 0.0)
    before = jnp.dot(tri_ref[...], self.astype(BF16), preferred_element_type=F32) + carry[...]
    slot_ref[...] = jnp.where(sel, before, -1.0)
    carry[...] = carry[...] + jnp.sum(self, axis=0, keepdims=True)
    cnt_ref[0] = jnp.broadcast_to(carry[...], cnt_ref.shape[1:])


def _finish(xp2, groups_p, yconv_p, xs2, groups_s, yconv_s, w, seq, tile):
    n_p, n_s = xp2.shape[0], xs2.shape[0]
    tm = FIN_TILE
    p_steps, s_steps = n_p // tm, n_s // tm
    steps = p_steps + s_steps
    n = n_p + n_s
    per_b = seq // tm
    prow = lambda width: pl.BlockSpec((tm, width), lambda i: (jnp.minimum(i, p_steps - 1), 0))
    srow = lambda width: pl.BlockSpec((tm, width), lambda i: (jnp.maximum(i - p_steps, 0), 0))

    def pgroup(d):
        def index(i):
            ip = jnp.minimum(i, p_steps - 1)
            return (ip // per_b, 0, ip % per_b, 0)
        return pl.BlockSpec((1, d, tm // d, ATTN_WIDTH), index)

    row = lambda width: pl.BlockSpec((tm, width), lambda i: (i, 0))
    weights = (w["g1"], w["w_gate"], w["b_gate"], w["w_ab"], w["w_cb"], w["w_out"], w["g2"],
               w["w_router"], w["b_router"], w["tri"])
    return pl.pallas_call(
        functools.partial(_finish_kernel, tm=tm, p_steps=p_steps, tile=tile),
        grid=(steps,),
        in_specs=[prow(D_MODEL)] + [pgroup(d) for d in DILATIONS] * 2 + [prow(CONV_CH)]
        + [srow(D_MODEL)] + [srow(ATTN_WIDTH)] * 7 + [_resident(a.shape) for a in weights],
        out_specs=[row(D_MODEL), row(D_MODEL), row(LANES), row(LANES),
                   pl.BlockSpec((1, 8, LANES), lambda i: (i, 0, 0))],
        out_shape=[jax.ShapeDtypeStruct((n, D_MODEL), F32), jax.ShapeDtypeStruct((n, D_MODEL), BF16),
                   jax.ShapeDtypeStruct((n, LANES), F32), jax.ShapeDtypeStruct((n, LANES), F32),
                   jax.ShapeDtypeStruct((steps, 8, LANES), F32)],
        scratch_shapes=[pltpu.VMEM((tm, D_MODEL), F32)] + [pltpu.VMEM((LANE_TILES, tm, LANES), F32)] * 6
        + [pltpu.VMEM((tm, CONV_CH), F32), pltpu.VMEM((1, LANES), F32)],
        compiler_params=_params("arbitrary"),
        name="finish",
    )(xp2, *[o for o, _ in groups_p], *[ld for _, ld in groups_p], yconv_p,
      xs2, *[o for o, _ in groups_s], *[ld for _, ld in groups_s], yconv_s, *weights)


def _moe_kernel(cnt_ref, h_ref, x2_ref, slot_ref, gate_ref, wgu_ref, bgu_ref, wdn_ref, bdn_ref,
                yp_ref, ys_ref, *, tile, chunk, tiles, n_s):
    t = pl.program_id(0)
    e = pl.program_id(1)

    @pl.when(e == 0)
    def _():
        yp_ref[...] = x2_ref[...]

    count = cnt_ref[t * N_EXPERTS + e]
    slots = slot_ref[0, pl.ds(e, 1), :]
    gates = gate_ref[0, pl.ds(e, 1), :]

    def body(c, carry):
        want = (lax.broadcasted_iota(jnp.int32, (chunk, tile), 0) + c * chunk).astype(F32)
        hit = slots == want
        onehot = jnp.where(hit, 1.0, 0.0).astype(BF16)
        row_gate = jnp.sum(jnp.where(hit, gates, 0.0), axis=-1, keepdims=True)
        xg = jnp.dot(onehot, h_ref[...], preferred_element_type=F32).astype(BF16)
        gu = jnp.dot(xg, wgu_ref[0], preferred_element_type=F32) + bgu_ref[0]
        glu = jnp.minimum(gu[:, :D_FF], SWIGLU_LIMIT)
        lin = jnp.clip(gu[:, D_FF:], -SWIGLU_LIMIT, SWIGLU_LIMIT)
        act = glu * _sigmoid(SWIGLU_ALPHA * glu) * (lin + 1.0)
        yb = (jnp.dot(act.astype(BF16), wdn_ref[0], preferred_element_type=F32) + bdn_ref[0]) * row_gate
        yp_ref[...] += lax.dot_general(onehot, yb.astype(BF16), (((0,), (0,)), ((), ())),
                                       preferred_element_type=F32)
        return carry

    lax.fori_loop(0, (count + chunk - 1) // chunk, body, 0)

    @pl.when((e == N_EXPERTS - 1) & (t == tiles - 1))
    def _():
        ys_ref[...] = yp_ref[tile - n_s:, :]


def _experts(counts, h2, x2, slot_t, gate_t, w_gu, b_gu, w_dn, b_dn, tile, n_p, n_s):
    n = h2.shape[0]
    tiles = n // tile
    chunk = min(MOE_CHUNK, tile)
    assert n_s <= tile and (n_s % 8 == 0)
    once = dict(pipeline_mode=pl.Buffered(1))
    grid_spec = pltpu.PrefetchScalarGridSpec(
        num_scalar_prefetch=1,
        grid=(tiles, N_EXPERTS),
        in_specs=[
            pl.BlockSpec((tile, D_MODEL), lambda t, e, c: (t, 0), **once),
            pl.BlockSpec((tile, D_MODEL), lambda t, e, c: (t, 0), **once),
            pl.BlockSpec((1, N_EXPERTS, tile), lambda t, e, c: (t, 0, 0)),
            pl.BlockSpec((1, N_EXPERTS, tile), lambda t, e, c: (t, 0, 0)),
            pl.BlockSpec((1, D_MODEL, 2 * D_FF), lambda t, e, c: (e, 0, 0)),
            pl.BlockSpec((1, 1, 2 * D_FF), lambda t, e, c: (e, 0, 0)),
            pl.BlockSpec((1, D_FF, D_MODEL), lambda t, e, c: (e, 0, 0)),
            pl.BlockSpec((1, 1, D_MODEL), lambda t, e, c: (e, 0, 0)),
        ],
        out_specs=[pl.BlockSpec((tile, D_MODEL), lambda t, e, c: (t, 0)),
                   pl.BlockSpec((n_s, D_MODEL), lambda t, e, c: (0, 0))],
    )
    return pl.pallas_call(
        functools.partial(_moe_kernel, tile=tile, chunk=chunk, tiles=tiles, n_s=n_s),
        grid_spec=grid_spec,
        out_shape=[jax.ShapeDtypeStruct((n_p, D_MODEL), F32), jax.ShapeDtypeStruct((n_s, D_MODEL), F32)],
        compiler_params=_params("arbitrary", "arbitrary"),
        name="experts",
    )(counts, h2, x2, slot_t, gate_t, w_gu, b_gu, w_dn, b_dn)


def _moe_tile(n, n_s):
    for tile in (MOE_TILE, 1024, 512):
        if n % tile == 0 and n_s <= tile:
            return tile
    raise ValueError(f"no expert tile divides {n} tokens")


def _layer(xp, xs, kv_caches, conv_state, p):
    batch, seq, _ = xp.shape
    db, t_new, _ = xs.shape
    n_p, n_s = batch * seq, db * t_new
    tile = _moe_tile(n_p + n_s, n_s)
    assert seq % (DILATIONS[-1] * ATTN_BLOCK) == 0 and seq >= WINDOWS[-1]
    assert n_p % FIN_TILE == 0 and n_s % FIN_TILE == 0 and tile % FIN_TILE == 0 and seq % FIN_TILE == 0
    assert t_new <= min(d for d in DILATIONS if d > 1)
    for g in range(N_GROUPS):
        assert kv_caches[g].shape[1] == WINDOWS[g]

    w_in = p["w_in"]
    w_qkv_glu = w_in[:, :ALL_QKV + 2 * CONV_CH].astype(BF16)
    qk_gain = jnp.zeros((8, ATTN_WIDTH), F32)
    qk_gain = qk_gain.at[0].set(jnp.tile(p["q_norm_gain"], HEADS)).at[1].set(jnp.tile(p["k_norm_gain"], HEADS))
    head_of = jnp.arange(MXU_DIM) // HEAD_DIM
    bd = (head_of[:, None] == head_of[None, :]).astype(BF16)
    g1 = p["norm1_gain"][None, :]
    tri = (jnp.arange(FIN_TILE)[:, None] > jnp.arange(FIN_TILE)[None, :]).astype(BF16)
    wfin = dict(
        g1=g1, w_gate=w_in[:, ALL_QKV + 2 * CONV_CH:].astype(BF16), b_gate=p["b_gate"].reshape(1, 2 * D_MODEL),
        w_ab=p["w_attn_branch"].astype(BF16), w_cb=p["w_conv_branch"].astype(BF16),
        w_out=p["w_out"].astype(BF16), g2=p["norm2_gain"][None, :],
        w_router=jnp.pad(p["w_router"], ((0, 0), (0, LANES - N_EXPERTS))),
        b_router=jnp.pad(p["b_router"], (0, LANES - N_EXPERTS))[None, :], tri=tri)
    cvec = lambda a: a[None, :]
    conv_args = (p["conv_w"], cvec(p["conv_b"]), cvec(p["conv_ln_gain"]), cvec(p["conv_ln_bias"]))
    slopes = _alibi_slopes()

    xp2 = xp.reshape(n_p, D_MODEL)
    *zg, up = _in_projection(xp2, g1, w_qkv_glu, qk_gain, bd, batch, seq, DILATIONS)
    groups_p = [_prompt_attention(zg[g], _prompt_bias(slopes[g], DILATIONS[g])) for g in range(N_GROUPS)]
    yconv_p = _prompt_conv(up, *conv_args, batch, seq)
    kv_p = []
    for g in range(N_GROUPS):
        tail = zg[g][:, :, seq // DILATIONS[g] - ATTN_BLOCK:, ATTN_WIDTH:]
        tail = jnp.swapaxes(tail, 1, 2).reshape(batch, WINDOWS[g], 2, HEADS, HEAD_DIM)
        kv_p.append(tail[None])
    conv_p = up.reshape(batch, seq, CONV_CH)[:, seq - (CONV_WIDTH - 1):][None]

    xs2 = xs.reshape(n_s, D_MODEL)
    *zsg, us = _in_projection(xs2, g1, w_qkv_glu, qk_gain, bd, 1, n_s, (1, 1, 1))
    zs3 = jnp.concatenate([z.reshape(db, t_new, QKV_COLS) for z in zsg], axis=2)
    kv_s, groups_s = [], []
    for g in range(N_GROUPS):
        new_cache, o, ld = _sample_cache_step(kv_caches[g], zs3, g, slopes[g], t_new)
        kv_s.append(new_cache[None])
        groups_s.append((o, ld))
    u_ctx = jnp.concatenate([conv_state, us.reshape(db, t_new, CONV_CH)], axis=1)
    yconv_s = _sample_conv(jnp.swapaxes(u_ctx, 0, 1), *conv_args, t_new)
    yconv_s = jnp.swapaxes(yconv_s, 0, 1).reshape(n_s, CONV_CH)
    conv_s = u_ctx[:, t_new:][None]

    x2, h2, slot, gate, cnt = _finish(xp2, groups_p, yconv_p, xs2, groups_s, yconv_s, wfin, seq, tile)
    n = n_p + n_s
    tiles = n // tile

    def tile_major(a):
        return jnp.swapaxes(a.reshape(tiles, tile, LANES), 1, 2)[:, :N_EXPERTS]

    per_tile = tile // FIN_TILE
    counts = cnt[per_tile - 1::per_tile, 0, :N_EXPERTS].astype(jnp.int32).reshape(-1)
    y_p, y_s = _experts(counts, h2, x2, tile_major(slot), tile_major(gate), p["w_gate_up"].astype(BF16),
                        p["b_gate_up"][:, None, :], p["w_down"].astype(BF16), p["b_down"][:, None, :],
                        tile, n_p, n_s)
    return y_p.reshape(batch, seq, D_MODEL), y_s.reshape(db, t_new, D_MODEL), kv_p, conv_p, kv_s, conv_s


def kernel(x_prompt, x_sample, cache_kv_w128, cache_kv_w512, cache_kv_w2048, state_conv, norm1_gain, w_in,
           q_norm_gain, k_norm_gain, b_gate, conv_w, conv_b, conv_ln_gain, conv_ln_bias, w_attn_branch,
           w_conv_branch, w_out, norm2_gain, w_router, b_router, w_gate_up, b_gate_up, w_down, b_down):
    depth = w_in.shape[0]
    assert depth == 1, "one layer per step"
    params = dict(norm1_gain=norm1_gain, w_in=w_in, q_norm_gain=q_norm_gain, k_norm_gain=k_norm_gain,
                  b_gate=b_gate, conv_w=conv_w, conv_b=conv_b, conv_ln_gain=conv_ln_gain,
                  conv_ln_bias=conv_ln_bias, w_attn_branch=w_attn_branch, w_conv_branch=w_conv_branch,
                  w_out=w_out, norm2_gain=norm2_gain, w_router=w_router, b_router=b_router,
                  w_gate_up=w_gate_up, b_gate_up=b_gate_up, w_down=w_down, b_down=b_down)
    p = {k: v[0] for k, v in params.items()}
    caches = (cache_kv_w128[0], cache_kv_w512[0], cache_kv_w2048[0])
    y_p, y_s, kv_p, conv_p, kv_s, conv_s = _layer(x_prompt, x_sample, caches, state_conv[0], p)
    return (y_p, y_s, kv_p[0], kv_p[1], kv_p[2], conv_p, kv_s[0], kv_s[1], kv_s[2], conv_s)
```

```python
import functools

import jax
import jax.numpy as jnp
from jax import lax
from jax.experimental import pallas as pl
from jax.experimental.pallas import tpu as pltpu

F32 = jnp.float32
BF16 = jnp.bfloat16

D_MODEL = 1024
HEAD_DIM = 64
HEADS = 8
ATTN_WIDTH = HEADS * HEAD_DIM
N_GROUPS = 3
WINDOWS = (128, 512, 2048)
DILATIONS = (1, 4, 16)
ATTN_BLOCK = 128
QKV_COLS = 3 * ATTN_WIDTH
ALL_QKV = N_GROUPS * QKV_COLS
CONV_CH = 512
CONV_WIDTH = 31
CONV_HALO = 32
N_EXPERTS = 32
TOP_K = 4
D_FF = 1024
SWIGLU_ALPHA = 1.702
SWIGLU_LIMIT = 7.0
NORM_EPS = 1e-6
NEG_INF = -1e30
FAR = 1e30

LANES = 128
SUBLANES = 8
LANE_TILES = ATTN_WIDTH // LANES
MXU_DIM = 256
VMEM_LIMIT = 56 * 1024 * 1024

IN_TILE = 512
ATTN_TILE = 512
CONV_TILE = 256
CONV_CHUNK = 64
FIN_TILE = 256
MOE_TILE = 1536
MOE_CHUNK = 224


def _sigmoid(x):
    return 1.0 / (1.0 + jnp.exp(-x))


def _rms(x, gain):
    ms = jnp.mean(x * x, axis=-1, keepdims=True)
    return x * lax.rsqrt(ms + NORM_EPS) * gain


def _split_bf16(w):
    hi = w.astype(BF16)
    return jnp.stack([hi, (w - hi.astype(F32)).astype(BF16)])


def _params(*sem):
    return pltpu.CompilerParams(dimension_semantics=sem, vmem_limit_bytes=VMEM_LIMIT)


def _resident(shape):
    return pl.BlockSpec(shape, lambda *_: (0,) * len(shape), pipeline_mode=pl.Buffered(1))


def _alibi_slopes():
    n = N_GROUPS * HEADS
    s = 2.0 ** (-8.0 * jnp.arange(1, n + 1, dtype=F32) / n)
    return s.reshape(N_GROUPS, HEADS)


def _inproj_kernel(x_ref, g1_ref, w_ref, qk_ref, bd_ref, z0_ref, z1_ref, z2_ref, u_ref, scr, *, tm, dils):
    h = _rms(x_ref[...], g1_ref[...]).astype(BF16)
    bd = bd_ref[...]
    outs = (z0_ref, z1_ref, z2_ref)
    for g in range(N_GROUPS):
        d = dils[g]
        for part in range(3):
            col = g * QKV_COLS + part * ATTN_WIDTH
            z = jnp.dot(h, w_ref[:, col:col + ATTN_WIDTH], preferred_element_type=F32)
            if part < 2:
                zz = (z * z).astype(BF16)
                ss = jnp.concatenate(
                    [jnp.dot(zz[:, :MXU_DIM], bd, preferred_element_type=F32),
                     jnp.dot(zz[:, MXU_DIM:], bd, preferred_element_type=F32)], axis=1)
                z = z * lax.rsqrt(ss * (1.0 / HEAD_DIM) + NORM_EPS) * qk_ref[part:part + 1, :]
            cols = slice(part * ATTN_WIDTH, (part + 1) * ATTN_WIDTH)
            if d == 1:
                outs[g][0, 0, :, cols] = z
            else:
                stage = scr.at[part]
                for c in range(LANE_TILES):
                    stage[c] = z[:, c * LANES:(c + 1) * LANES]
                for r in range(d):
                    for c in range(LANE_TILES):
                        lanes = slice(part * ATTN_WIDTH + c * LANES, part * ATTN_WIDTH + (c + 1) * LANES)
                        outs[g][0, r, :, lanes] = stage.at[c][pl.ds(r, tm // d, stride=d), :]
    a = jnp.dot(h, w_ref[:, ALL_QKV:ALL_QKV + CONV_CH], preferred_element_type=F32)
    gate = jnp.dot(h, w_ref[:, ALL_QKV + CONV_CH:], preferred_element_type=F32)
    u_ref[...] = a * _sigmoid(gate)


def _in_projection(x2d, g1, w_bf, qk_gain, bd, batch, seq, dils):
    n = x2d.shape[0]
    tm = min(IN_TILE, seq)
    per_b = seq // tm
    assert all(tm % (8 * d) == 0 for d in dils)

    def zspec(d):
        return pl.BlockSpec((1, d, tm // d, QKV_COLS), lambda i: (i // per_b, 0, i % per_b, 0))

    return pl.pallas_call(
        functools.partial(_inproj_kernel, tm=tm, dils=dils),
        grid=(n // tm,),
        in_specs=[
            pl.BlockSpec((tm, D_MODEL), lambda i: (i, 0)),
            _resident((1, D_MODEL)),
            _resident(w_bf.shape),
            _resident((8, ATTN_WIDTH)),
            _resident((MXU_DIM, MXU_DIM)),
        ],
        out_specs=[zspec(d) for d in dils] + [pl.BlockSpec((tm, CONV_CH), lambda i: (i, 0))],
        out_shape=[jax.ShapeDtypeStruct((batch, d, seq // d, QKV_COLS), F32) for d in dils]
        + [jax.ShapeDtypeStruct((n, CONV_CH), F32)],
        scratch_shapes=[pltpu.VMEM((3, LANE_TILES, tm, LANES), F32)],
        compiler_params=_params("parallel"),
        name="in_projection",
    )(x2d, g1, w_bf, qk_gain, bd)


def _attn_kernel(q_ref, kc_ref, vc_ref, kp_ref, vp_ref, bias_ref, o_ref, ld_ref, kf, vf, *, tq):
    n = pl.program_id(2)
    kf[0:ATTN_BLOCK, :] = kp_ref[0, 0].astype(BF16)
    kf[ATTN_BLOCK:, :] = kc_ref[0, 0].astype(BF16)
    vf[0:ATTN_BLOCK, :] = vp_ref[0, 0].astype(BF16)
    vf[ATTN_BLOCK:, :] = vc_ref[0, 0].astype(BF16)
    lo = lax.broadcasted_iota(jnp.int32, (ATTN_BLOCK, LANES), 1) < HEAD_DIM
    for i in range(tq // ATTN_BLOCK):
        rows = slice(i * ATTN_BLOCK, (i + 1) * ATTN_BLOCK)
        krows = slice(i * ATTN_BLOCK, (i + 2) * ATTN_BLOCK)
        table = jnp.where(n == 0, 0, 1) if i == 0 else 1
        for hp in range(HEADS // 2):
            cols = slice(hp * LANES, (hp + 1) * LANES)
            qp = q_ref[0, 0, rows, cols] * (HEAD_DIM ** -0.5)
            qq = jnp.concatenate([jnp.where(lo, qp, 0.0), jnp.where(lo, 0.0, qp)], axis=0).astype(BF16)
            s = lax.dot_general(qq, kf[krows, cols], (((1,), (1,)), ((), ())),
                                preferred_element_type=F32)
            s = s + bias_ref[table, hp]
            m = jnp.max(s, axis=-1, keepdims=True)
            p = jnp.exp(s - m)
            l = jnp.sum(p, axis=-1, keepdims=True)
            o2 = jnp.dot(p.astype(BF16), vf[krows, cols], preferred_element_type=F32) / l
            ld2 = m + jnp.log(l)
            o_ref[0, 0, rows, cols] = jnp.where(lo, o2[:ATTN_BLOCK], o2[ATTN_BLOCK:])
            ld_ref[0, 0, rows, cols] = jnp.where(lo, ld2[:ATTN_BLOCK], ld2[ATTN_BLOCK:])


def _prompt_attention(zg, bias):
    batch, d, sub, _ = zg.shape
    tq = min(ATTN_TILE, sub)
    per = tq // ATTN_BLOCK

    def cur(part):
        return pl.BlockSpec((1, 1, tq, ATTN_WIDTH), lambda b, r, n: (b, r, n, part))

    def prev(part):
        return pl.BlockSpec((1, 1, ATTN_BLOCK, ATTN_WIDTH),
                            lambda b, r, n: (b, r, jnp.maximum(n * per - 1, 0), part))

    return pl.pallas_call(
        functools.partial(_attn_kernel, tq=tq),
        grid=(batch, d, sub // tq),
        in_specs=[cur(0), cur(1), cur(2), prev(1), prev(2), _resident(bias.shape)],
        out_specs=[cur(0), cur(0)],
        out_shape=[jax.ShapeDtypeStruct((batch, d, sub, ATTN_WIDTH), F32)] * 2,
        scratch_shapes=[pltpu.VMEM((tq + ATTN_BLOCK, ATTN_WIDTH), BF16)] * 2,
        compiler_params=_params("parallel", "parallel", "arbitrary"),
        name=f"prompt_attention_d{d}",
    )(zg, zg, zg, zg, zg, bias)


def _prompt_bias(slopes, d):
    i = jnp.arange(ATTN_BLOCK)[:, None]
    j = jnp.arange(2 * ATTN_BLOCK)[None, :]
    steps = i + ATTN_BLOCK - j
    valid = (steps >= 0) & (steps <= ATTN_BLOCK)
    first = valid & (j >= ATTN_BLOCK)
    bias = -slopes[:, None, None] * (steps * d).astype(F32)
    tables = jnp.stack([jnp.where(first[None], bias, NEG_INF), jnp.where(valid[None], bias, NEG_INF)])
    return tables.reshape(2, HEADS // 2, 2 * ATTN_BLOCK, 2 * ATTN_BLOCK)


def _conv_post(y, cb, lg, lb):
    y = y + cb
    mu = jnp.mean(y, axis=-1, keepdims=True)
    yc = y - mu
    y = yc * lax.rsqrt(jnp.mean(yc * yc, axis=-1, keepdims=True) + NORM_EPS) * lg + lb
    return y * _sigmoid(y)


def _conv_kernel(uc_ref, up_ref, cw_ref, cb_ref, lg_ref, lb_ref, y_ref, buf, phased, *, tt):
    n = pl.program_id(1)
    buf[0:CONV_HALO, :] = jnp.where(n == 0, 0.0, up_ref[0])
    buf[CONV_HALO:, :] = uc_ref[0]
    for s in range(SUBLANES):
        span = tt + CONV_HALO - (SUBLANES if s else 0)
        phased[s, 0:span, :] = buf[s:s + span, :]
    lead = CONV_HALO - (CONV_WIDTH - 1)
    for c in range(tt // CONV_CHUNK):
        r0 = c * CONV_CHUNK
        acc = jnp.zeros((CONV_CHUNK, CONV_CH), F32)
        for w in range(CONV_WIDTH):
            phase = (lead + w) % SUBLANES
            base = r0 + lead + w - phase
            acc = acc + phased[phase, base:base + CONV_CHUNK, :] * cw_ref[w:w + 1, :]
        y_ref[0, r0:r0 + CONV_CHUNK, :] = _conv_post(acc, cb_ref[...], lg_ref[...], lb_ref[...])


def _prompt_conv(u, conv_w, conv_b, ln_g, ln_b, batch, seq):
    tt = min(CONV_TILE, seq)
    u3 = u.reshape(batch, seq, CONV_CH)
    y = pl.pallas_call(
        functools.partial(_conv_kernel, tt=tt),
        grid=(batch, seq // tt),
        in_specs=[
            pl.BlockSpec((1, tt, CONV_CH), lambda b, n: (b, n, 0)),
            pl.BlockSpec((1, CONV_HALO, CONV_CH),
                         lambda b, n: (b, jnp.maximum(n * (tt // CONV_HALO) - 1, 0), 0)),
            _resident((CONV_WIDTH, CONV_CH)), _resident((1, CONV_CH)), _resident((1, CONV_CH)),
            _resident((1, CONV_CH)),
        ],
        out_specs=pl.BlockSpec((1, tt, CONV_CH), lambda b, n: (b, n, 0)),
        out_shape=jax.ShapeDtypeStruct((batch, seq, CONV_CH), F32),
        scratch_shapes=[pltpu.VMEM((tt + CONV_HALO, CONV_CH), F32),
                        pltpu.VMEM((SUBLANES, tt + CONV_HALO, CONV_CH), F32)],
        compiler_params=_params("parallel", "arbitrary"),
        name="prompt_conv",
    )(u3, u3, conv_w, conv_b, ln_g, ln_b)
    return y.reshape(batch * seq, CONV_CH)


def _sample_conv_kernel(ctx_ref, cw_ref, cb_ref, lg_ref, lb_ref, y_ref, *, t_new):
    for t in range(t_new):
        acc = jnp.zeros(y_ref.shape[1:], F32)
        for w in range(CONV_WIDTH):
            acc = acc + ctx_ref[t + w] * cw_ref[w:w + 1, :]
        y_ref[t] = _conv_post(acc, cb_ref[...], lg_ref[...], lb_ref[...])


def _sample_conv(ctx_t, conv_w, conv_b, ln_g, ln_b, t_new):
    rows, db, _ = ctx_t.shape
    return pl.pallas_call(
        functools.partial(_sample_conv_kernel, t_new=t_new),
        grid=(1,),
        in_specs=[_resident((rows, db, CONV_CH)), _resident((CONV_WIDTH, CONV_CH)), _resident((1, CONV_CH)),
                  _resident((1, CONV_CH)), _resident((1, CONV_CH))],
        out_specs=pl.BlockSpec((t_new, db, CONV_CH), lambda i: (0, 0, 0)),
        out_shape=jax.ShapeDtypeStruct((t_new, db, CONV_CH), F32),
        compiler_params=_params("arbitrary"),
        name="sample_conv",
    )(ctx_t, conv_w, conv_b, ln_g, ln_b)


def _sample_cache_kernel(c_ref, new_ref, z_ref, bold_ref, bnew_ref, out_ref, o_ref, ld_ref, kb, vb,
                         *, length, t_new, g):
    lane = lax.broadcasted_iota(jnp.int32, (HEAD_DIM, LANES), 1)
    is_new = lane >= LANES - t_new
    for kv, dst in ((0, kb), (1, vb)):
        for h in range(HEADS):
            rows = slice(h * HEAD_DIM, (h + 1) * HEAD_DIM)
            old = c_ref[0, kv, rows, :]
            dst[rows, :] = old.astype(BF16)
            moved = pltpu.roll(old, length - t_new, 1)
            if length > LANES:
                out_ref[0, kv, rows, :length - LANES] = moved[:, :length - LANES]
            out_ref[0, kv, rows, length - LANES:] = jnp.where(is_new, new_ref[0, kv, rows, :],
                                                              moved[:, length - LANES:])
    zq = z_ref[0, :, g * QKV_COLS:g * QKV_COLS + ATTN_WIDTH] * (HEAD_DIM ** -0.5)
    row_head = lax.broadcasted_iota(jnp.int32, (HEADS, ATTN_WIDTH), 0)
    lane_head = lax.broadcasted_iota(jnp.int32, (HEADS, ATTN_WIDTH), 1) // HEAD_DIM
    own = row_head == lane_head
    qbd = jnp.concatenate([jnp.where(own, zq[j:j + 1, :], 0.0) for j in range(t_new)], axis=0).astype(BF16)
    k_new = new_ref[0, 0].astype(BF16)
    v_new = new_ref[0, 1].astype(BF16)
    s_old = jnp.dot(qbd, kb[...], preferred_element_type=F32) + bold_ref[...]
    s_new = jnp.dot(qbd, k_new, preferred_element_type=F32) + bnew_ref[...]
    m = jnp.maximum(jnp.max(s_old, axis=-1, keepdims=True), jnp.max(s_new, axis=-1, keepdims=True))
    p_old = jnp.exp(s_old - m)
    p_new = jnp.exp(s_new - m)
    l = jnp.sum(p_old, axis=-1, keepdims=True) + jnp.sum(p_new, axis=-1, keepdims=True)
    contract_lanes = (((1,), (1,)), ((), ()))
    acc = (lax.dot_general(p_old.astype(BF16), vb[...], contract_lanes, preferred_element_type=F32)
           + lax.dot_general(p_new.astype(BF16), v_new, contract_lanes, preferred_element_type=F32))
    o_rows = acc / l
    ld_rows = m + jnp.log(l)
    for j in range(t_new):
        rows = slice(j * HEADS, (j + 1) * HEADS)
        o_ref[0, j:j + 1, :] = jnp.sum(jnp.where(own, o_rows[rows], 0.0), axis=0, keepdims=True)
        ld_ref[0, j:j + 1, :] = jnp.sum(jnp.where(own, ld_rows[rows], 0.0), axis=0, keepdims=True)


def _sample_bias(slopes_g, length, d, t_new):
    j = jnp.arange(t_new)[:, None]
    back = length + j - jnp.arange(length)[None, :]
    ok = (back % d == 0) & (back <= ATTN_BLOCK * d)
    d_old = jnp.where(ok, back.astype(F32), FAR)
    jj = jnp.arange(LANES)[None, :] - (LANES - t_new)
    back = j - jj
    ok = (jj >= 0) & (back >= 0) & (back % d == 0)
    d_new = jnp.where(ok, back.astype(F32), FAR)
    expand = lambda dist: (-slopes_g[None, :, None] * dist[:, None, :]).reshape(t_new * HEADS, dist.shape[-1])
    return expand(d_old), expand(d_new)


def _sample_cache_step(cache, zs3, g, slopes_g, t_new):
    db, length = cache.shape[:2]
    d = DILATIONS[g]
    cache_t = jnp.transpose(cache, (0, 2, 3, 4, 1)).reshape(db, 2, ATTN_WIDTH, length)
    kv_new = zs3[:, :, g * QKV_COLS + ATTN_WIDTH:(g + 1) * QKV_COLS].reshape(db, t_new, 2, ATTN_WIDTH)
    new_t = jnp.pad(jnp.transpose(kv_new, (0, 2, 3, 1)), ((0, 0),) * 3 + ((LANES - t_new, 0),))
    b_old, b_new = _sample_bias(slopes_g, length, d, t_new)
    blk = lambda last: pl.BlockSpec((1, 2, ATTN_WIDTH, last), lambda b: (b, 0, 0, 0))
    row_spec = pl.BlockSpec((1, t_new, ATTN_WIDTH), lambda b: (b, 0, 0))
    out_t, o, ld = pl.pallas_call(
        functools.partial(_sample_cache_kernel, length=length, t_new=t_new, g=g),
        grid=(db,),
        in_specs=[blk(length), blk(LANES), pl.BlockSpec((1, t_new, ALL_QKV), lambda b: (b, 0, 0)),
                  _resident(b_old.shape), _resident(b_new.shape)],
        out_specs=[blk(length), row_spec, row_spec],
        out_shape=[jax.ShapeDtypeStruct((db, 2, ATTN_WIDTH, length), F32),
                   jax.ShapeDtypeStruct((db, t_new, ATTN_WIDTH), F32),
                   jax.ShapeDtypeStruct((db, t_new, ATTN_WIDTH), F32)],
        scratch_shapes=[pltpu.VMEM((ATTN_WIDTH, length), BF16)] * 2,
        compiler_params=_params("parallel"),
        name=f"sample_cache_d{d}",
    )(cache_t, new_t, zs3, b_old, b_new)
    new_cache = jnp.transpose(out_t.reshape(db, 2, HEADS, HEAD_DIM, length), (0, 4, 1, 2, 3))
    return new_cache, o.reshape(db * t_new, ATTN_WIDTH), ld.reshape(db * t_new, ATTN_WIDTH)


def _finish_kernel(*refs, tm, p_steps, tile):
    (xp, op0, op1, op2, lp0, lp1, lp2, ycp, xs, os0, os1, os2, ls0, ls1, ls2, ycs,
     g1_ref, wg_ref, bg_ref, wab_ref, wcb_ref, wo_ref, g2_ref, wr_ref, br_ref, tri_ref,
     x2_ref, h2_ref, slot_ref, gate_ref, cnt_ref,
     xb, ob0, ob1, ob2, lb0, lb1, lb2, ycb, carry) = refs
    i = pl.program_id(0)
    bufs = (xb, ob0, ob1, ob2, lb0, lb1, lb2, ycb)

    @pl.when(i < p_steps)
    def _():
        xb[...] = xp[...]
        ycb[...] = ycp[...]
        for g, (src_o, src_l) in enumerate(((op0, lp0), (op1, lp1), (op2, lp2))):
            d = DILATIONS[g]
            for src, dst in ((src_o, bufs[1 + g]), (src_l, bufs[4 + g])):
                for c in range(LANE_TILES):
                    lanes = slice(c * LANES, (c + 1) * LANES)
                    for r in range(d):
                        dst.at[c][pl.ds(r, tm // d, stride=d), :] = src[0, r, :, lanes]

    @pl.when(i >= p_steps)
    def _():
        xb[...] = xs[...]
        ycb[...] = ycs[...]
        for src, dst in zip((os0, os1, os2, ls0, ls1, ls2), bufs[1:7]):
            for c in range(LANE_TILES):
                dst[c] = src[:, c * LANES:(c + 1) * LANES]

    @pl.when((i * tm) % tile == 0)
    def _():
        carry[...] = jnp.zeros_like(carry)

    wide = lambda buf: jnp.concatenate([buf[c] for c in range(LANE_TILES)], axis=1)
    l0, l1, l2 = wide(lb0), wide(lb1), wide(lb2)
    m = jnp.maximum(jnp.maximum(l0, l1), l2)
    e0, e1, e2 = jnp.exp(l0 - m), jnp.exp(l1 - m), jnp.exp(l2 - m)
    y_attn = (e0 * wide(ob0) + e1 * wide(ob1) + e2 * wide(ob2)) / (e0 + e1 + e2)

    x = xb[...]
    ya = jnp.dot(y_attn.astype(BF16), wab_ref[...], preferred_element_type=F32)
    yc = jnp.dot(ycb[...].astype(BF16), wcb_ref[...], preferred_element_type=F32)
    h = _rms(x, g1_ref[...]).astype(BF16)
    gates = jnp.dot(h, wg_ref[...], preferred_element_type=F32) + bg_ref[...]
    mixed = _sigmoid(gates[:, :D_MODEL]) * ya + _sigmoid(gates[:, D_MODEL:]) * yc
    x2 = x + jnp.dot(mixed.astype(BF16), wo_ref[...], preferred_element_type=F32)
    x2_ref[...] = x2
    h2 = _rms(x2, g2_ref[...])
    h2_ref[...] = h2.astype(BF16)

    h2_hi = h2.astype(BF16)
    h2_lo = (h2 - h2_hi.astype(F32)).astype(BF16)
    logits = (jnp.dot(h2_hi, wr_ref[0], preferred_element_type=F32)
              + jnp.dot(h2_lo, wr_ref[0], preferred_element_type=F32)
              + jnp.dot(h2_hi, wr_ref[1], preferred_element_type=F32)) + br_ref[...]
    lane = lax.broadcasted_iota(jnp.int32, logits.shape, 1).astype(F32)
    work = jnp.where(lane < N_EXPERTS, logits, -jnp.inf)
    sel = jnp.zeros(logits.shape, jnp.bool_)
    top = None
    den = jnp.zeros((tm, 1), F32)
    for _ in range(TOP_K):
        mx = jnp.max(work, axis=-1, keepdims=True)
        idx = jnp.min(jnp.where(work == mx, lane, float(LANES)), axis=-1, keepdims=True)
        pick = lane == idx
        sel = sel | pick
        top = mx if top is None else top
        den = den + jnp.exp(mx - top)
        work = jnp.where(pick, -jnp.inf, work)
    gate_ref[...] = jnp.where(sel, jnp.exp(logits - top) / den, 0.0)

    self = jnp.where(sel, 1.0, 0.0)
    before = jnp.dot(tri_ref[...], self.astype(BF16), preferred_element_type=F32) + carry[...]
    slot_ref[...] = jnp.where(sel, before, -1.0)
    carry[...] = carry[...] + jnp.sum(self, axis=0, keepdims=True)
    cnt_ref[0] = jnp.broadcast_to(carry[...], cnt_ref.shape[1:])


def _finish(xp2, groups_p, yconv_p, xs2, groups_s, yconv_s, w, seq, tile):
    n_p, n_s = xp2.shape[0], xs2.shape[0]
    tm = FIN_TILE
    p_steps, s_steps = n_p // tm, n_s // tm
    steps = p_steps + s_steps
    n = n_p + n_s
    per_b = seq // tm
    prow = lambda width: pl.BlockSpec((tm, width), lambda i: (jnp.minimum(i, p_steps - 1), 0))
    srow = lambda width: pl.BlockSpec((tm, width), lambda i: (jnp.maximum(i - p_steps, 0), 0))

    def pgroup(d):
        def index(i):
            ip = jnp.minimum(i, p_steps - 1)
            return (ip // per_b, 0, ip % per_b, 0)
        return pl.BlockSpec((1, d, tm // d, ATTN_WIDTH), index)

    row = lambda width: pl.BlockSpec((tm, width), lambda i: (i, 0))
    weights = (w["g1"], w["w_gate"], w["b_gate"], w["w_ab"], w["w_cb"], w["w_out"], w["g2"],
               w["w_router"], w["b_router"], w["tri"])
    return pl.pallas_call(
        functools.partial(_finish_kernel, tm=tm, p_steps=p_steps, tile=tile),
        grid=(steps,),
        in_specs=[prow(D_MODEL)] + [pgroup(d) for d in DILATIONS] * 2 + [prow(CONV_CH)]
        + [srow(D_MODEL)] + [srow(ATTN_WIDTH)] * 7 + [_resident(a.shape) for a in weights],
        out_specs=[row(D_MODEL), row(D_MODEL), row(LANES), row(LANES),
                   pl.BlockSpec((1, 8, LANES), lambda i: (i, 0, 0))],
        out_shape=[jax.ShapeDtypeStruct((n, D_MODEL), F32), jax.ShapeDtypeStruct((n, D_MODEL), BF16),
                   jax.ShapeDtypeStruct((n, LANES), F32), jax.ShapeDtypeStruct((n, LANES), F32),
                   jax.ShapeDtypeStruct((steps, 8, LANES), F32)],
        scratch_shapes=[pltpu.VMEM((tm, D_MODEL), F32)] + [pltpu.VMEM((LANE_TILES, tm, LANES), F32)] * 6
        + [pltpu.VMEM((tm, CONV_CH), F32), pltpu.VMEM((1, LANES), F32)],
        compiler_params=_params("arbitrary"),
        name="finish",
    )(xp2, *[o for o, _ in groups_p], *[ld for _, ld in groups_p], yconv_p,
      xs2, *[o for o, _ in groups_s], *[ld for _, ld in groups_s], yconv_s, *weights)


def _moe_kernel(cnt_ref, h_ref, x2_ref, slot_ref, gate_ref, wgu_ref, bgu_ref, wdn_ref, bdn_ref,
                yp_ref, ys_ref, *, tile, chunk, tiles, n_s):
    t = pl.program_id(0)
    e = pl.program_id(1)

    @pl.when(e == 0)
    def _():
        yp_ref[...] = x2_ref[...]

    count = cnt_ref[t * N_EXPERTS + e]
    slots = slot_ref[0, pl.ds(e, 1), :]
    gates = gate_ref[0, pl.ds(e, 1), :]

    def body(c, carry):
        want = (lax.broadcasted_iota(jnp.int32, (chunk, tile), 0) + c * chunk).astype(F32)
        hit = slots == want
        onehot = jnp.where(hit, 1.0, 0.0).astype(BF16)
        row_gate = jnp.sum(jnp.where(hit, gates, 0.0), axis=-1, keepdims=True)
        xg = jnp.dot(onehot, h_ref[...], preferred_element_type=F32).astype(BF16)
        gu = jnp.dot(xg, wgu_ref[0], preferred_element_type=F32) + bgu_ref[0]
        glu = jnp.minimum(gu[:, :D_FF], SWIGLU_LIMIT)
        lin = jnp.clip(gu[:, D_FF:], -SWIGLU_LIMIT, SWIGLU_LIMIT)
        act = glu * _sigmoid(SWIGLU_ALPHA * glu) * (lin + 1.0)
        yb = (jnp.dot(act.astype(BF16), wdn_ref[0], preferred_element_type=F32) + bdn_ref[0]) * row_gate
        yp_ref[...] += lax.dot_general(onehot, yb.astype(BF16), (((0,), (0,)), ((), ())),
                                       preferred_element_type=F32)
        return carry

    body(0, 0)
    lax.fori_loop(1, (count + chunk - 1) // chunk, body, 0)

    @pl.when((e == N_EXPERTS - 1) & (t == tiles - 1))
    def _():
        ys_ref[...] = yp_ref[tile - n_s:, :]


def _experts(counts, h2, x2, slot_t, gate_t, w_gu, b_gu, w_dn, b_dn, tile, n_p, n_s):
    n = h2.shape[0]
    tiles = n // tile
    chunk = min(MOE_CHUNK, tile)
    assert n_s <= tile and (n_s % 8 == 0)
    once = dict(pipeline_mode=pl.Buffered(1))
    grid_spec = pltpu.PrefetchScalarGridSpec(
        num_scalar_prefetch=1,
        grid=(tiles, N_EXPERTS),
        in_specs=[
            pl.BlockSpec((tile, D_MODEL), lambda t, e, c: (t, 0), **once),
            pl.BlockSpec((tile, D_MODEL), lambda t, e, c: (t, 0), **once),
            pl.BlockSpec((1, N_EXPERTS, tile), lambda t, e, c: (t, 0, 0)),
            pl.BlockSpec((1, N_EXPERTS, tile), lambda t, e, c: (t, 0, 0)),
            pl.BlockSpec((1, D_MODEL, 2 * D_FF), lambda t, e, c: (e, 0, 0)),
            pl.BlockSpec((1, 1, 2 * D_FF), lambda t, e, c: (e, 0, 0)),
            pl.BlockSpec((1, D_FF, D_MODEL), lambda t, e, c: (e, 0, 0)),
            pl.BlockSpec((1, 1, D_MODEL), lambda t, e, c: (e, 0, 0)),
        ],
        out_specs=[pl.BlockSpec((tile, D_MODEL), lambda t, e, c: (t, 0)),
                   pl.BlockSpec((n_s, D_MODEL), lambda t, e, c: (0, 0))],
    )
    return pl.pallas_call(
        functools.partial(_moe_kernel, tile=tile, chunk=chunk, tiles=tiles, n_s=n_s),
        grid_spec=grid_spec,
        out_shape=[jax.ShapeDtypeStruct((n_p, D_MODEL), F32), jax.ShapeDtypeStruct((n_s, D_MODEL), F32)],
        compiler_params=_params("arbitrary", "arbitrary"),
        name="experts",
    )(counts, h2, x2, slot_t, gate_t, w_gu, b_gu, w_dn, b_dn)


def _moe_tile(n, n_s):
    for tile in (MOE_TILE, 1024, 512):
        if n % tile == 0 and n_s <= tile:
            return tile
    raise ValueError(f"no expert tile divides {n} tokens")


def _layer(xp, xs, kv_caches, conv_state, p):
    batch, seq, _ = xp.shape
    db, t_new, _ = xs.shape
    n_p, n_s = batch * seq, db * t_new
    tile = _moe_tile(n_p + n_s, n_s)
    assert seq % (DILATIONS[-1] * ATTN_BLOCK) == 0 and seq >= WINDOWS[-1]
    assert n_p % FIN_TILE == 0 and n_s % FIN_TILE == 0 and tile % FIN_TILE == 0 and seq % FIN_TILE == 0
    assert t_new <= min(d for d in DILATIONS if d > 1)
    for g in range(N_GROUPS):
        assert kv_caches[g].shape[1] == WINDOWS[g]

    w_in = p["w_in"]
    w_qkv_glu = w_in[:, :ALL_QKV + 2 * CONV_CH].astype(BF16)
    qk_gain = jnp.zeros((8, ATTN_WIDTH), F32)
    qk_gain = qk_gain.at[0].set(jnp.tile(p["q_norm_gain"], HEADS)).at[1].set(jnp.tile(p["k_norm_gain"], HEADS))
    head_of = jnp.arange(MXU_DIM) // HEAD_DIM
    bd = (head_of[:, None] == head_of[None, :]).astype(BF16)
    g1 = p["norm1_gain"][None, :]
    tri = (jnp.arange(FIN_TILE)[:, None] > jnp.arange(FIN_TILE)[None, :]).astype(BF16)
    wfin = dict(
        g1=g1, w_gate=w_in[:, ALL_QKV + 2 * CONV_CH:].astype(BF16), b_gate=p["b_gate"].reshape(1, 2 * D_MODEL),
        w_ab=p["w_attn_branch"].astype(BF16), w_cb=p["w_conv_branch"].astype(BF16),
        w_out=p["w_out"].astype(BF16), g2=p["norm2_gain"][None, :],
        w_router=_split_bf16(jnp.pad(p["w_router"], ((0, 0), (0, LANES - N_EXPERTS)))),
        b_router=jnp.pad(p["b_router"], (0, LANES - N_EXPERTS))[None, :], tri=tri)
    cvec = lambda a: a[None, :]
    conv_args = (p["conv_w"], cvec(p["conv_b"]), cvec(p["conv_ln_gain"]), cvec(p["conv_ln_bias"]))
    slopes = _alibi_slopes()

    xp2 = xp.reshape(n_p, D_MODEL)
    *zg, up = _in_projection(xp2, g1, w_qkv_glu, qk_gain, bd, batch, seq, DILATIONS)
    groups_p = [_prompt_attention(zg[g], _prompt_bias(slopes[g], DILATIONS[g])) for g in range(N_GROUPS)]
    yconv_p = _prompt_conv(up, *conv_args, batch, seq)
    kv_p = []
    for g in range(N_GROUPS):
        tail = zg[g][:, :, seq // DILATIONS[g] - ATTN_BLOCK:, ATTN_WIDTH:]
        tail = jnp.swapaxes(tail, 1, 2).reshape(batch, WINDOWS[g], 2, HEADS, HEAD_DIM)
        kv_p.append(tail[None])
    conv_p = up.reshape(batch, seq, CONV_CH)[:, seq - (CONV_WIDTH - 1):][None]

    xs2 = xs.reshape(n_s, D_MODEL)
    *zsg, us = _in_projection(xs2, g1, w_qkv_glu, qk_gain, bd, 1, n_s, (1, 1, 1))
    zs3 = jnp.concatenate([z.reshape(db, t_new, QKV_COLS) for z in zsg], axis=2)
    kv_s, groups_s = [], []
    for g in range(N_GROUPS):
        new_cache, o, ld = _sample_cache_step(kv_caches[g], zs3, g, slopes[g], t_new)
        kv_s.append(new_cache[None])
        groups_s.append((o, ld))
    u_ctx = jnp.concatenate([conv_state, us.reshape(db, t_new, CONV_CH)], axis=1)
    yconv_s = _sample_conv(jnp.swapaxes(u_ctx, 0, 1), *conv_args, t_new)
    yconv_s = jnp.swapaxes(yconv_s, 0, 1).reshape(n_s, CONV_CH)
    conv_s = u_ctx[:, t_new:][None]

    x2, h2, slot, gate, cnt = _finish(xp2, groups_p, yconv_p, xs2, groups_s, yconv_s, wfin, seq, tile)
    n = n_p + n_s
    tiles = n // tile

    def tile_major(a):
        return jnp.swapaxes(a.reshape(tiles, tile, LANES), 1, 2)[:, :N_EXPERTS]

    per_tile = tile // FIN_TILE
    counts = cnt[per_tile - 1::per_tile, 0, :N_EXPERTS].astype(jnp.int32).reshape(-1)
    y_p, y_s = _experts(counts, h2, x2, tile_major(slot), tile_major(gate), p["w_gate_up"].astype(BF16),
                        p["b_gate_up"][:, None, :], p["w_down"].astype(BF16), p["b_down"][:, None, :],
                        tile, n_p, n_s)
    return y_p.reshape(batch, seq, D_MODEL), y_s.reshape(db, t_new, D_MODEL), kv_p, conv_p, kv_s, conv_s


def kernel(x_prompt, x_sample, cache_kv_w128, cache_kv_w512, cache_kv_w2048, state_conv, norm1_gain, w_in,
           q_norm_gain, k_norm_gain, b_gate, conv_w, conv_b, conv_ln_gain, conv_ln_bias, w_attn_branch,
           w_conv_branch, w_out, norm2_gain, w_router, b_router, w_gate_up, b_gate_up, w_down, b_down):
    depth = w_in.shape[0]
    assert depth == 1, "one layer per step"
    params = dict(norm1_gain=norm1_gain, w_in=w_in, q_norm_gain=q_norm_gain, k_norm_gain=k_norm_gain,
                  b_gate=b_gate, conv_w=conv_w, conv_b=conv_b, conv_ln_gain=conv_ln_gain,
                  conv_ln_bias=conv_ln_bias, w_attn_branch=w_attn_branch, w_conv_branch=w_conv_branch,
                  w_out=w_out, norm2_gain=norm2_gain, w_router=w_router, b_router=b_router,
                  w_gate_up=w_gate_up, b_gate_up=b_gate_up, w_down=w_down, b_down=b_down)
    p = {k: v[0] for k, v in params.items()}
    caches = (cache_kv_w128[0], cache_kv_w512[0], cache_kv_w2048[0])
    y_p, y_s, kv_p, conv_p, kv_s, conv_s = _layer(x_prompt, x_sample, caches, state_conv[0], p)
    return (y_p, y_s, kv_p[0], kv_p[1], kv_p[2], conv_p, kv_s[0], kv_s[1], kv_s[2], conv_s)
```

```python
import functools

import jax
import jax.numpy as jnp
from jax import lax
from jax.experimental import pallas as pl
from jax.experimental.pallas import tpu as pltpu

F32 = jnp.float32
BF16 = jnp.bfloat16

D_MODEL = 1024
HEAD_DIM = 64
HEADS = 8
ATTN_WIDTH = HEADS * HEAD_DIM
N_GROUPS = 3
WINDOWS = (128, 512, 2048)
DILATIONS = (1, 4, 16)
ATTN_BLOCK = 128
QKV_COLS = 3 * ATTN_WIDTH
ALL_QKV = N_GROUPS * QKV_COLS
CONV_CH = 512
CONV_WIDTH = 31
CONV_HALO = 32
N_EXPERTS = 32
TOP_K = 4
D_FF = 1024
SWIGLU_ALPHA = 1.702
SWIGLU_LIMIT = 7.0
NORM_EPS = 1e-6
NEG_INF = -1e30
FAR = 1e30

LANES = 128
SUBLANES = 8
LANE_TILES = ATTN_WIDTH // LANES
MXU_DIM = 256
VMEM_LIMIT = 56 * 1024 * 1024

IN_TILE = 512
ATTN_TILE = 512
CONV_TILE = 256
CONV_CHUNK = 64
FIN_TILE = 256
MOE_TILE = 1536
MOE_CHUNK = 224


def _sigmoid(x):
    return 1.0 / (1.0 + jnp.exp(-x))


def _rms(x, gain):
    ms = jnp.mean(x * x, axis=-1, keepdims=True)
    return x * lax.rsqrt(ms + NORM_EPS) * gain


def _split_bf16(w):
    hi = w.astype(BF16)
    return jnp.stack([hi, (w - hi.astype(F32)).astype(BF16)])


def _params(*sem):
    return pltpu.CompilerParams(dimension_semantics=sem, vmem_limit_bytes=VMEM_LIMIT)


def _resident(shape):
    return pl.BlockSpec(shape, lambda *_: (0,) * len(shape), pipeline_mode=pl.Buffered(1))


def _alibi_slopes():
    n = N_GROUPS * HEADS
    s = 2.0 ** (-8.0 * jnp.arange(1, n + 1, dtype=F32) / n)
    return s.reshape(N_GROUPS, HEADS)


def _inproj_kernel(x_ref, g1_ref, w_ref, qk_ref, bd_ref, z0_ref, z1_ref, z2_ref, u_ref, scr, *, tm, dils):
    h = _rms(x_ref[...], g1_ref[...]).astype(BF16)
    bd = bd_ref[...]
    outs = (z0_ref, z1_ref, z2_ref)
    for g in range(N_GROUPS):
        d = dils[g]
        for part in range(3):
            col = g * QKV_COLS + part * ATTN_WIDTH
            z = jnp.dot(h, w_ref[:, col:col + ATTN_WIDTH], preferred_element_type=F32)
            if part < 2:
                zz = (z * z).astype(BF16)
                ss = jnp.concatenate(
                    [jnp.dot(zz[:, :MXU_DIM], bd, preferred_element_type=F32),
                     jnp.dot(zz[:, MXU_DIM:], bd, preferred_element_type=F32)], axis=1)
                z = z * lax.rsqrt(ss * (1.0 / HEAD_DIM) + NORM_EPS) * qk_ref[part:part + 1, :]
            cols = slice(part * ATTN_WIDTH, (part + 1) * ATTN_WIDTH)
            if d == 1:
                outs[g][0, 0, :, cols] = z
            else:
                stage = scr.at[part]
                for c in range(LANE_TILES):
                    stage[c] = z[:, c * LANES:(c + 1) * LANES]
                for r in range(d):
                    for c in range(LANE_TILES):
                        lanes = slice(part * ATTN_WIDTH + c * LANES, part * ATTN_WIDTH + (c + 1) * LANES)
                        outs[g][0, r, :, lanes] = stage.at[c][pl.ds(r, tm // d, stride=d), :]
    a = jnp.dot(h, w_ref[:, ALL_QKV:ALL_QKV + CONV_CH], preferred_element_type=F32)
    gate = jnp.dot(h, w_ref[:, ALL_QKV + CONV_CH:], preferred_element_type=F32)
    u_ref[...] = a * _sigmoid(gate)


def _in_projection(x2d, g1, w_bf, qk_gain, bd, batch, seq, dils):
    n = x2d.shape[0]
    tm = min(IN_TILE, seq)
    per_b = seq // tm
    assert all(tm % (8 * d) == 0 for d in dils)

    def zspec(d):
        return pl.BlockSpec((1, d, tm // d, QKV_COLS), lambda i: (i // per_b, 0, i % per_b, 0))

    return pl.pallas_call(
        functools.partial(_inproj_kernel, tm=tm, dils=dils),
        grid=(n // tm,),
        in_specs=[
            pl.BlockSpec((tm, D_MODEL), lambda i: (i, 0)),
            _resident((1, D_MODEL)),
            _resident(w_bf.shape),
            _resident((8, ATTN_WIDTH)),
            _resident((MXU_DIM, MXU_DIM)),
        ],
        out_specs=[zspec(d) for d in dils] + [pl.BlockSpec((tm, CONV_CH), lambda i: (i, 0))],
        out_shape=[jax.ShapeDtypeStruct((batch, d, seq // d, QKV_COLS), F32) for d in dils]
        + [jax.ShapeDtypeStruct((n, CONV_CH), F32)],
        scratch_shapes=[pltpu.VMEM((3, LANE_TILES, tm, LANES), F32)],
        compiler_params=_params("parallel"),
        name="in_projection",
    )(x2d, g1, w_bf, qk_gain, bd)


def _attn_kernel(q_ref, kc_ref, vc_ref, kp_ref, vp_ref, bias_ref, o_ref, ld_ref, kf, vf, *, tq):
    n = pl.program_id(2)
    kf[0:ATTN_BLOCK, :] = kp_ref[0, 0].astype(BF16)
    kf[ATTN_BLOCK:, :] = kc_ref[0, 0].astype(BF16)
    vf[0:ATTN_BLOCK, :] = vp_ref[0, 0].astype(BF16)
    vf[ATTN_BLOCK:, :] = vc_ref[0, 0].astype(BF16)
    lo = lax.broadcasted_iota(jnp.int32, (ATTN_BLOCK, LANES), 1) < HEAD_DIM
    for i in range(tq // ATTN_BLOCK):
        rows = slice(i * ATTN_BLOCK, (i + 1) * ATTN_BLOCK)
        krows = slice(i * ATTN_BLOCK, (i + 2) * ATTN_BLOCK)
        table = jnp.where(n == 0, 0, 1) if i == 0 else 1
        for hp in range(HEADS // 2):
            cols = slice(hp * LANES, (hp + 1) * LANES)
            qp = q_ref[0, 0, rows, cols] * (HEAD_DIM ** -0.5)
            qq = jnp.concatenate([jnp.where(lo, qp, 0.0), jnp.where(lo, 0.0, qp)], axis=0).astype(BF16)
            s = lax.dot_general(qq, kf[krows, cols], (((1,), (1,)), ((), ())),
                                preferred_element_type=F32)
            s = s + bias_ref[table, hp]
            m = jnp.max(s, axis=-1, keepdims=True)
            p = jnp.exp(s - m)
            l = jnp.sum(p, axis=-1, keepdims=True)
            o2 = jnp.dot(p.astype(BF16), vf[krows, cols], preferred_element_type=F32) / l
            ld2 = m + jnp.log(l)
            o_ref[0, 0, rows, cols] = jnp.where(lo, o2[:ATTN_BLOCK], o2[ATTN_BLOCK:]).astype(o_ref.dtype)
            ld_ref[0, 0, rows, cols] = jnp.where(lo, ld2[:ATTN_BLOCK], ld2[ATTN_BLOCK:])


def _prompt_attention(zg, bias):
    batch, d, sub, _ = zg.shape
    tq = min(ATTN_TILE, sub)
    per = tq // ATTN_BLOCK

    def cur(part):
        return pl.BlockSpec((1, 1, tq, ATTN_WIDTH), lambda b, r, n: (b, r, n, part))

    def prev(part):
        return pl.BlockSpec((1, 1, ATTN_BLOCK, ATTN_WIDTH),
                            lambda b, r, n: (b, r, jnp.maximum(n * per - 1, 0), part))

    return pl.pallas_call(
        functools.partial(_attn_kernel, tq=tq),
        grid=(batch, d, sub // tq),
        in_specs=[cur(0), cur(1), cur(2), prev(1), prev(2), _resident(bias.shape)],
        out_specs=[cur(0), cur(0)],
        out_shape=[jax.ShapeDtypeStruct((batch, d, sub, ATTN_WIDTH), BF16),
                   jax.ShapeDtypeStruct((batch, d, sub, ATTN_WIDTH), F32)],
        scratch_shapes=[pltpu.VMEM((tq + ATTN_BLOCK, ATTN_WIDTH), BF16)] * 2,
        compiler_params=_params("parallel", "parallel", "arbitrary"),
        name=f"prompt_attention_d{d}",
    )(zg, zg, zg, zg, zg, bias)


def _prompt_bias(slopes, d):
    i = jnp.arange(ATTN_BLOCK)[:, None]
    j = jnp.arange(2 * ATTN_BLOCK)[None, :]
    steps = i + ATTN_BLOCK - j
    valid = (steps >= 0) & (steps <= ATTN_BLOCK)
    first = valid & (j >= ATTN_BLOCK)
    bias = -slopes[:, None, None] * (steps * d).astype(F32)
    tables = jnp.stack([jnp.where(first[None], bias, NEG_INF), jnp.where(valid[None], bias, NEG_INF)])
    return tables.reshape(2, HEADS // 2, 2 * ATTN_BLOCK, 2 * ATTN_BLOCK)


def _conv_post(y, cb, lg, lb):
    y = y + cb
    mu = jnp.mean(y, axis=-1, keepdims=True)
    yc = y - mu
    y = yc * lax.rsqrt(jnp.mean(yc * yc, axis=-1, keepdims=True) + NORM_EPS) * lg + lb
    return y * _sigmoid(y)


def _conv_kernel(uc_ref, up_ref, cw_ref, cb_ref, lg_ref, lb_ref, y_ref, buf, phased, *, tt):
    n = pl.program_id(1)
    buf[0:CONV_HALO, :] = jnp.where(n == 0, 0.0, up_ref[0])
    buf[CONV_HALO:, :] = uc_ref[0]
    for s in range(SUBLANES):
        span = tt + CONV_HALO - (SUBLANES if s else 0)
        phased[s, 0:span, :] = buf[s:s + span, :]
    lead = CONV_HALO - (CONV_WIDTH - 1)
    for c in range(tt // CONV_CHUNK):
        r0 = c * CONV_CHUNK
        acc = jnp.zeros((CONV_CHUNK, CONV_CH), F32)
        for w in range(CONV_WIDTH):
            phase = (lead + w) % SUBLANES
            base = r0 + lead + w - phase
            acc = acc + phased[phase, base:base + CONV_CHUNK, :] * cw_ref[w:w + 1, :]
        y_ref[0, r0:r0 + CONV_CHUNK, :] = _conv_post(acc, cb_ref[...], lg_ref[...], lb_ref[...])


def _prompt_conv(u, conv_w, conv_b, ln_g, ln_b, batch, seq):
    tt = min(CONV_TILE, seq)
    u3 = u.reshape(batch, seq, CONV_CH)
    y = pl.pallas_call(
        functools.partial(_conv_kernel, tt=tt),
        grid=(batch, seq // tt),
        in_specs=[
            pl.BlockSpec((1, tt, CONV_CH), lambda b, n: (b, n, 0)),
            pl.BlockSpec((1, CONV_HALO, CONV_CH),
                         lambda b, n: (b, jnp.maximum(n * (tt // CONV_HALO) - 1, 0), 0)),
            _resident((CONV_WIDTH, CONV_CH)), _resident((1, CONV_CH)), _resident((1, CONV_CH)),
            _resident((1, CONV_CH)),
        ],
        out_specs=pl.BlockSpec((1, tt, CONV_CH), lambda b, n: (b, n, 0)),
        out_shape=jax.ShapeDtypeStruct((batch, seq, CONV_CH), F32),
        scratch_shapes=[pltpu.VMEM((tt + CONV_HALO, CONV_CH), F32),
                        pltpu.VMEM((SUBLANES, tt + CONV_HALO, CONV_CH), F32)],
        compiler_params=_params("parallel", "arbitrary"),
        name="prompt_conv",
    )(u3, u3, conv_w, conv_b, ln_g, ln_b)
    return y.reshape(batch * seq, CONV_CH)


def _sample_conv_kernel(ctx_ref, cw_ref, cb_ref, lg_ref, lb_ref, y_ref, *, t_new):
    for t in range(t_new):
        acc = jnp.zeros(y_ref.shape[1:], F32)
        for w in range(CONV_WIDTH):
            acc = acc + ctx_ref[t + w] * cw_ref[w:w + 1, :]
        y_ref[t] = _conv_post(acc, cb_ref[...], lg_ref[...], lb_ref[...])


def _sample_conv(ctx_t, conv_w, conv_b, ln_g, ln_b, t_new):
    rows, db, _ = ctx_t.shape
    return pl.pallas_call(
        functools.partial(_sample_conv_kernel, t_new=t_new),
        grid=(1,),
        in_specs=[_resident((rows, db, CONV_CH)), _resident((CONV_WIDTH, CONV_CH)), _resident((1, CONV_CH)),
                  _resident((1, CONV_CH)), _resident((1, CONV_CH))],
        out_specs=pl.BlockSpec((t_new, db, CONV_CH), lambda i: (0, 0, 0)),
        out_shape=jax.ShapeDtypeStruct((t_new, db, CONV_CH), F32),
        compiler_params=_params("arbitrary"),
        name="sample_conv",
    )(ctx_t, conv_w, conv_b, ln_g, ln_b)


def _sample_cache_kernel(c_ref, new_ref, z_ref, bold_ref, bnew_ref, out_ref, o_ref, ld_ref, kb, vb,
                         *, length, t_new, g):
    lane = lax.broadcasted_iota(jnp.int32, (HEAD_DIM, LANES), 1)
    is_new = lane >= LANES - t_new
    for kv, dst in ((0, kb), (1, vb)):
        for h in range(HEADS):
            rows = slice(h * HEAD_DIM, (h + 1) * HEAD_DIM)
            old = c_ref[0, kv, rows, :]
            dst[rows, :] = old.astype(BF16)
            moved = pltpu.roll(old, length - t_new, 1)
            if length > LANES:
                out_ref[0, kv, rows, :length - LANES] = moved[:, :length - LANES]
            out_ref[0, kv, rows, length - LANES:] = jnp.where(is_new, new_ref[0, kv, rows, :],
                                                              moved[:, length - LANES:])
    zq = z_ref[0, :, g * QKV_COLS:g * QKV_COLS + ATTN_WIDTH] * (HEAD_DIM ** -0.5)
    row_head = lax.broadcasted_iota(jnp.int32, (HEADS, ATTN_WIDTH), 0)
    lane_head = lax.broadcasted_iota(jnp.int32, (HEADS, ATTN_WIDTH), 1) // HEAD_DIM
    own = row_head == lane_head
    qbd = jnp.concatenate([jnp.where(own, zq[j:j + 1, :], 0.0) for j in range(t_new)], axis=0).astype(BF16)
    k_new = new_ref[0, 0].astype(BF16)
    v_new = new_ref[0, 1].astype(BF16)
    s_old = jnp.dot(qbd, kb[...], preferred_element_type=F32) + bold_ref[...]
    s_new = jnp.dot(qbd, k_new, preferred_element_type=F32) + bnew_ref[...]
    m = jnp.maximum(jnp.max(s_old, axis=-1, keepdims=True), jnp.max(s_new, axis=-1, keepdims=True))
    p_old = jnp.exp(s_old - m)
    p_new = jnp.exp(s_new - m)
    l = jnp.sum(p_old, axis=-1, keepdims=True) + jnp.sum(p_new, axis=-1, keepdims=True)
    contract_lanes = (((1,), (1,)), ((), ()))
    acc = (lax.dot_general(p_old.astype(BF16), vb[...], contract_lanes, preferred_element_type=F32)
           + lax.dot_general(p_new.astype(BF16), v_new, contract_lanes, preferred_element_type=F32))
    o_rows = acc / l
    ld_rows = m + jnp.log(l)
    for j in range(t_new):
        rows = slice(j * HEADS, (j + 1) * HEADS)
        o_ref[0, j:j + 1, :] = jnp.sum(jnp.where(own, o_rows[rows], 0.0), axis=0, keepdims=True)
        ld_ref[0, j:j + 1, :] = jnp.sum(jnp.where(own, ld_rows[rows], 0.0), axis=0, keepdims=True)


def _sample_bias(slopes_g, length, d, t_new):
    j = jnp.arange(t_new)[:, None]
    back = length + j - jnp.arange(length)[None, :]
    ok = (back % d == 0) & (back <= ATTN_BLOCK * d)
    d_old = jnp.where(ok, back.astype(F32), FAR)
    jj = jnp.arange(LANES)[None, :] - (LANES - t_new)
    back = j - jj
    ok = (jj >= 0) & (back >= 0) & (back % d == 0)
    d_new = jnp.where(ok, back.astype(F32), FAR)
    expand = lambda dist: (-slopes_g[None, :, None] * dist[:, None, :]).reshape(t_new * HEADS, dist.shape[-1])
    return expand(d_old), expand(d_new)


def _sample_cache_step(cache, zs3, g, slopes_g, t_new):
    db, length = cache.shape[:2]
    d = DILATIONS[g]
    cache_t = jnp.transpose(cache, (0, 2, 3, 4, 1)).reshape(db, 2, ATTN_WIDTH, length)
    kv_new = zs3[:, :, g * QKV_COLS + ATTN_WIDTH:(g + 1) * QKV_COLS].reshape(db, t_new, 2, ATTN_WIDTH)
    new_t = jnp.pad(jnp.transpose(kv_new, (0, 2, 3, 1)), ((0, 0),) * 3 + ((LANES - t_new, 0),))
    b_old, b_new = _sample_bias(slopes_g, length, d, t_new)
    blk = lambda last: pl.BlockSpec((1, 2, ATTN_WIDTH, last), lambda b: (b, 0, 0, 0))
    row_spec = pl.BlockSpec((1, t_new, ATTN_WIDTH), lambda b: (b, 0, 0))
    out_t, o, ld = pl.pallas_call(
        functools.partial(_sample_cache_kernel, length=length, t_new=t_new, g=g),
        grid=(db,),
        in_specs=[blk(length), blk(LANES), pl.BlockSpec((1, t_new, ALL_QKV), lambda b: (b, 0, 0)),
                  _resident(b_old.shape), _resident(b_new.shape)],
        out_specs=[blk(length), row_spec, row_spec],
        out_shape=[jax.ShapeDtypeStruct((db, 2, ATTN_WIDTH, length), F32),
                   jax.ShapeDtypeStruct((db, t_new, ATTN_WIDTH), F32),
                   jax.ShapeDtypeStruct((db, t_new, ATTN_WIDTH), F32)],
        scratch_shapes=[pltpu.VMEM((ATTN_WIDTH, length), BF16)] * 2,
        compiler_params=_params("parallel"),
        name=f"sample_cache_d{d}",
    )(cache_t, new_t, zs3, b_old, b_new)
    new_cache = jnp.transpose(out_t.reshape(db, 2, HEADS, HEAD_DIM, length), (0, 4, 1, 2, 3))
    return new_cache, o.reshape(db * t_new, ATTN_WIDTH), ld.reshape(db * t_new, ATTN_WIDTH)


def _finish_kernel(*refs, tm, p_steps, tile):
    (xp, op0, op1, op2, lp0, lp1, lp2, ycp, xs, os0, os1, os2, ls0, ls1, ls2, ycs,
     g1_ref, wg_ref, bg_ref, wab_ref, wcb_ref, wo_ref, g2_ref, wr_ref, br_ref, tri_ref,
     x2_ref, h2_ref, slot_ref, gate_ref, cnt_ref,
     xb, ob0, ob1, ob2, lb0, lb1, lb2, ycb, carry) = refs
    i = pl.program_id(0)
    bufs = (xb, ob0, ob1, ob2, lb0, lb1, lb2, ycb)

    @pl.when(i < p_steps)
    def _():
        xb[...] = xp[...]
        ycb[...] = ycp[...]
        for g, (src_o, src_l) in enumerate(((op0, lp0), (op1, lp1), (op2, lp2))):
            d = DILATIONS[g]
            for src, dst in ((src_o, bufs[1 + g]), (src_l, bufs[4 + g])):
                for c in range(LANE_TILES):
                    lanes = slice(c * LANES, (c + 1) * LANES)
                    for r in range(d):
                        dst.at[c][pl.ds(r, tm // d, stride=d), :] = src[0, r, :, lanes].astype(F32)

    @pl.when(i >= p_steps)
    def _():
        xb[...] = xs[...]
        ycb[...] = ycs[...]
        for src, dst in zip((os0, os1, os2, ls0, ls1, ls2), bufs[1:7]):
            for c in range(LANE_TILES):
                dst[c] = src[:, c * LANES:(c + 1) * LANES]

    @pl.when((i * tm) % tile == 0)
    def _():
        carry[...] = jnp.zeros_like(carry)

    wide = lambda buf: jnp.concatenate([buf[c] for c in range(LANE_TILES)], axis=1)
    l0, l1, l2 = wide(lb0), wide(lb1), wide(lb2)
    m = jnp.maximum(jnp.maximum(l0, l1), l2)
    e0, e1, e2 = jnp.exp(l0 - m), jnp.exp(l1 - m), jnp.exp(l2 - m)
    y_attn = (e0 * wide(ob0) + e1 * wide(ob1) + e2 * wide(ob2)) / (e0 + e1 + e2)

    x = xb[...]
    ya = jnp.dot(y_attn.astype(BF16), wab_ref[...], preferred_element_type=F32)
    yc = jnp.dot(ycb[...].astype(BF16), wcb_ref[...], preferred_element_type=F32)
    h = _rms(x, g1_ref[...]).astype(BF16)
    gates = jnp.dot(h, wg_ref[...], preferred_element_type=F32) + bg_ref[...]
    mixed = _sigmoid(gates[:, :D_MODEL]) * ya + _sigmoid(gates[:, D_MODEL:]) * yc
    x2 = x + jnp.dot(mixed.astype(BF16), wo_ref[...], preferred_element_type=F32)
    x2_ref[...] = x2
    h2 = _rms(x2, g2_ref[...])
    h2_ref[...] = h2.astype(BF16)

    h2_hi = h2.astype(BF16)
    h2_lo = (h2 - h2_hi.astype(F32)).astype(BF16)
    logits = (jnp.dot(h2_hi, wr_ref[0], preferred_element_type=F32)
              + jnp.dot(h2_lo, wr_ref[0], preferred_element_type=F32)
              + jnp.dot(h2_hi, wr_ref[1], preferred_element_type=F32)) + br_ref[...]
    lane = lax.broadcasted_iota(jnp.int32, logits.shape, 1).astype(F32)
    work = jnp.where(lane < N_EXPERTS, logits, -jnp.inf)
    sel = jnp.zeros(logits.shape, jnp.bool_)
    top = None
    den = jnp.zeros((tm, 1), F32)
    for _ in range(TOP_K):
        mx = jnp.max(work, axis=-1, keepdims=True)
        idx = jnp.min(jnp.where(work == mx, lane, float(LANES)), axis=-1, keepdims=True)
        pick = lane == idx
        sel = sel | pick
        top = mx if top is None else top
        den = den + jnp.exp(mx - top)
        work = jnp.where(pick, -jnp.inf, work)
    gate_ref[...] = jnp.where(sel, jnp.exp(logits - top) / den, 0.0)

    self = jnp.where(sel, 1.0, 0.0)
    before = jnp.dot(tri_ref[...], self.astype(BF16), preferred_element_type=F32) + carry[...]
    slot_ref[...] = jnp.where(sel, before, -1.0)
    carry[...] = carry[...] + jnp.sum(self, axis=0, keepdims=True)
    cnt_ref[0] = jnp.broadcast_to(carry[...], cnt_ref.shape[1:])


def _finish(xp2, groups_p, yconv_p, xs2, groups_s, yconv_s, w, seq, tile):
    n_p, n_s = xp2.shape[0], xs2.shape[0]
    tm = FIN_TILE
    p_steps, s_steps = n_p // tm, n_s // tm
    steps = p_steps + s_steps
    n = n_p + n_s
    per_b = seq // tm
    prow = lambda width: pl.BlockSpec((tm, width), lambda i: (jnp.minimum(i, p_steps - 1), 0))
    srow = lambda width: pl.BlockSpec((tm, width), lambda i: (jnp.maximum(i - p_steps, 0), 0))

    def pgroup(d):
        def index(i):
            ip = jnp.minimum(i, p_steps - 1)
            return (ip // per_b, 0, ip % per_b, 0)
        return pl.BlockSpec((1, d, tm // d, ATTN_WIDTH), index)

    row = lambda width: pl.BlockSpec((tm, width), lambda i: (i, 0))
    weights = (w["g1"], w["w_gate"], w["b_gate"], w["w_ab"], w["w_cb"], w["w_out"], w["g2"],
               w["w_router"], w["b_router"], w["tri"])
    return pl.pallas_call(
        functools.partial(_finish_kernel, tm=tm, p_steps=p_steps, tile=tile),
        grid=(steps,),
        in_specs=[prow(D_MODEL)] + [pgroup(d) for d in DILATIONS] * 2 + [prow(CONV_CH)]
        + [srow(D_MODEL)] + [srow(ATTN_WIDTH)] * 7 + [_resident(a.shape) for a in weights],
        out_specs=[row(D_MODEL), row(D_MODEL), row(LANES), row(LANES),
                   pl.BlockSpec((1, 8, LANES), lambda i: (i, 0, 0))],
        out_shape=[jax.ShapeDtypeStruct((n, D_MODEL), F32), jax.ShapeDtypeStruct((n, D_MODEL), BF16),
                   jax.ShapeDtypeStruct((n, LANES), F32), jax.ShapeDtypeStruct((n, LANES), F32),
                   jax.ShapeDtypeStruct((steps, 8, LANES), F32)],
        scratch_shapes=[pltpu.VMEM((tm, D_MODEL), F32)] + [pltpu.VMEM((LANE_TILES, tm, LANES), F32)] * 6
        + [pltpu.VMEM((tm, CONV_CH), F32), pltpu.VMEM((1, LANES), F32)],
        compiler_params=_params("arbitrary"),
        name="finish",
    )(xp2, *[o for o, _ in groups_p], *[ld for _, ld in groups_p], yconv_p,
      xs2, *[o for o, _ in groups_s], *[ld for _, ld in groups_s], yconv_s, *weights)


def _moe_kernel(cnt_ref, h_ref, x2_ref, slot_ref, gate_ref, wgu_ref, bgu_ref, wdn_ref, bdn_ref,
                yp_ref, ys_ref, *, tile, chunk, tiles, n_s):
    t = pl.program_id(0)
    e = pl.program_id(1)

    @pl.when(e == 0)
    def _():
        yp_ref[...] = x2_ref[...]

    count = cnt_ref[t * N_EXPERTS + e]
    slots = slot_ref[0, pl.ds(e, 1), :]
    gates = gate_ref[0, pl.ds(e, 1), :]

    def body(c, carry):
        want = (lax.broadcasted_iota(jnp.int32, (chunk, tile), 0) + c * chunk).astype(F32)
        hit = slots == want
        onehot = jnp.where(hit, 1.0, 0.0).astype(BF16)
        row_gate = jnp.sum(jnp.where(hit, gates, 0.0), axis=-1, keepdims=True)
        xg = jnp.dot(onehot, h_ref[...], preferred_element_type=F32).astype(BF16)
        gu = jnp.dot(xg, wgu_ref[0], preferred_element_type=F32) + bgu_ref[e]
        glu = jnp.minimum(gu[:, :D_FF], SWIGLU_LIMIT)
        lin = jnp.clip(gu[:, D_FF:], -SWIGLU_LIMIT, SWIGLU_LIMIT)
        act = glu * _sigmoid(SWIGLU_ALPHA * glu) * (lin + 1.0)
        yb = (jnp.dot(act.astype(BF16), wdn_ref[0], preferred_element_type=F32) + bdn_ref[e]) * row_gate
        yp_ref[...] += lax.dot_general(onehot, yb.astype(BF16), (((0,), (0,)), ((), ())),
                                       preferred_element_type=F32)
        return carry

    body(0, 0)
    lax.fori_loop(1, (count + chunk - 1) // chunk, body, 0)

    @pl.when((e == N_EXPERTS - 1) & (t == tiles - 1))
    def _():
        ys_ref[...] = yp_ref[tile - n_s:, :]


def _experts(counts, h2, x2, slot_t, gate_t, w_gu, b_gu, w_dn, b_dn, tile, n_p, n_s):
    n = h2.shape[0]
    tiles = n // tile
    chunk = min(MOE_CHUNK, tile)
    assert n_s <= tile and (n_s % 8 == 0)
    once = dict(pipeline_mode=pl.Buffered(1))
    grid_spec = pltpu.PrefetchScalarGridSpec(
        num_scalar_prefetch=1,
        grid=(tiles, N_EXPERTS),
        in_specs=[
            pl.BlockSpec((tile, D_MODEL), lambda t, e, c: (t, 0), **once),
            pl.BlockSpec((tile, D_MODEL), lambda t, e, c: (t, 0), **once),
            pl.BlockSpec((1, N_EXPERTS, tile), lambda t, e, c: (t, 0, 0)),
            pl.BlockSpec((1, N_EXPERTS, tile), lambda t, e, c: (t, 0, 0)),
            pl.BlockSpec((1, D_MODEL, 2 * D_FF), lambda t, e, c: (e, 0, 0)),
            _resident((N_EXPERTS, 1, 2 * D_FF)),
            pl.BlockSpec((1, D_FF, D_MODEL), lambda t, e, c: (e, 0, 0)),
            _resident((N_EXPERTS, 1, D_MODEL)),
        ],
        out_specs=[pl.BlockSpec((tile, D_MODEL), lambda t, e, c: (t, 0)),
                   pl.BlockSpec((n_s, D_MODEL), lambda t, e, c: (0, 0))],
    )
    return pl.pallas_call(
        functools.partial(_moe_kernel, tile=tile, chunk=chunk, tiles=tiles, n_s=n_s),
        grid_spec=grid_spec,
        out_shape=[jax.ShapeDtypeStruct((n_p, D_MODEL), F32), jax.ShapeDtypeStruct((n_s, D_MODEL), F32)],
        compiler_params=_params("arbitrary", "arbitrary"),
        name="experts",
    )(counts, h2, x2, slot_t, gate_t, w_gu, b_gu, w_dn, b_dn)


def _moe_tile(n, n_s):
    for tile in (MOE_TILE, 1024, 512):
        if n % tile == 0 and n_s <= tile:
            return tile
    raise ValueError(f"no expert tile divides {n} tokens")


def _layer(xp, xs, kv_caches, conv_state, p):
    batch, seq, _ = xp.shape
    db, t_new, _ = xs.shape
    n_p, n_s = batch * seq, db * t_new
    tile = _moe_tile(n_p + n_s, n_s)
    assert seq % (DILATIONS[-1] * ATTN_BLOCK) == 0 and seq >= WINDOWS[-1]
    assert n_p % FIN_TILE == 0 and n_s % FIN_TILE == 0 and tile % FIN_TILE == 0 and seq % FIN_TILE == 0
    assert t_new <= min(d for d in DILATIONS if d > 1)
    for g in range(N_GROUPS):
        assert kv_caches[g].shape[1] == WINDOWS[g]

    w_in = p["w_in"]
    w_qkv_glu = w_in[:, :ALL_QKV + 2 * CONV_CH].astype(BF16)
    qk_gain = jnp.zeros((8, ATTN_WIDTH), F32)
    qk_gain = qk_gain.at[0].set(jnp.tile(p["q_norm_gain"], HEADS)).at[1].set(jnp.tile(p["k_norm_gain"], HEADS))
    head_of = jnp.arange(MXU_DIM) // HEAD_DIM
    bd = (head_of[:, None] == head_of[None, :]).astype(BF16)
    g1 = p["norm1_gain"][None, :]
    tri = (jnp.arange(FIN_TILE)[:, None] > jnp.arange(FIN_TILE)[None, :]).astype(BF16)
    wfin = dict(
        g1=g1, w_gate=w_in[:, ALL_QKV + 2 * CONV_CH:].astype(BF16), b_gate=p["b_gate"].reshape(1, 2 * D_MODEL),
        w_ab=p["w_attn_branch"].astype(BF16), w_cb=p["w_conv_branch"].astype(BF16),
        w_out=p["w_out"].astype(BF16), g2=p["norm2_gain"][None, :],
        w_router=_split_bf16(jnp.pad(p["w_router"], ((0, 0), (0, LANES - N_EXPERTS)))),
        b_router=jnp.pad(p["b_router"], (0, LANES - N_EXPERTS))[None, :], tri=tri)
    cvec = lambda a: a[None, :]
    conv_args = (p["conv_w"], cvec(p["conv_b"]), cvec(p["conv_ln_gain"]), cvec(p["conv_ln_bias"]))
    slopes = _alibi_slopes()

    xp2 = xp.reshape(n_p, D_MODEL)
    *zg, up = _in_projection(xp2, g1, w_qkv_glu, qk_gain, bd, batch, seq, DILATIONS)
    groups_p = [_prompt_attention(zg[g], _prompt_bias(slopes[g], DILATIONS[g])) for g in range(N_GROUPS)]
    yconv_p = _prompt_conv(up, *conv_args, batch, seq)
    kv_p = []
    for g in range(N_GROUPS):
        tail = zg[g][:, :, seq // DILATIONS[g] - ATTN_BLOCK:, ATTN_WIDTH:]
        tail = jnp.swapaxes(tail, 1, 2).reshape(batch, WINDOWS[g], 2, HEADS, HEAD_DIM)
        kv_p.append(tail[None])
    conv_p = up.reshape(batch, seq, CONV_CH)[:, seq - (CONV_WIDTH - 1):][None]

    xs2 = xs.reshape(n_s, D_MODEL)
    *zsg, us = _in_projection(xs2, g1, w_qkv_glu, qk_gain, bd, 1, n_s, (1, 1, 1))
    zs3 = jnp.concatenate([z.reshape(db, t_new, QKV_COLS) for z in zsg], axis=2)
    kv_s, groups_s = [], []
    for g in range(N_GROUPS):
        new_cache, o, ld = _sample_cache_step(kv_caches[g], zs3, g, slopes[g], t_new)
        kv_s.append(new_cache[None])
        groups_s.append((o, ld))
    u_ctx = jnp.concatenate([conv_state, us.reshape(db, t_new, CONV_CH)], axis=1)
    yconv_s = _sample_conv(jnp.swapaxes(u_ctx, 0, 1), *conv_args, t_new)
    yconv_s = jnp.swapaxes(yconv_s, 0, 1).reshape(n_s, CONV_CH)
    conv_s = u_ctx[:, t_new:][None]

    x2, h2, slot, gate, cnt = _finish(xp2, groups_p, yconv_p, xs2, groups_s, yconv_s, wfin, seq, tile)
    n = n_p + n_s
    tiles = n // tile

    def tile_major(a):
        return jnp.swapaxes(a.reshape(tiles, tile, LANES), 1, 2)[:, :N_EXPERTS]

    per_tile = tile // FIN_TILE
    counts = cnt[per_tile - 1::per_tile, 0, :N_EXPERTS].astype(jnp.int32).reshape(-1)
    y_p, y_s = _experts(counts, h2, x2, tile_major(slot), tile_major(gate), p["w_gate_up"].astype(BF16),
                        p["b_gate_up"][:, None, :], p["w_down"].astype(BF16), p["b_down"][:, None, :],
                        tile, n_p, n_s)
    return y_p.reshape(batch, seq, D_MODEL), y_s.reshape(db, t_new, D_MODEL), kv_p, conv_p, kv_s, conv_s


def kernel(x_prompt, x_sample, cache_kv_w128, cache_kv_w512, cache_kv_w2048, state_conv, norm1_gain, w_in,
           q_norm_gain, k_norm_gain, b_gate, conv_w, conv_b, conv_ln_gain, conv_ln_bias, w_attn_branch,
           w_conv_branch, w_out, norm2_gain, w_router, b_router, w_gate_up, b_gate_up, w_down, b_down):
    depth = w_in.shape[0]
    assert depth == 1, "one layer per step"
    params = dict(norm1_gain=norm1_gain, w_in=w_in, q_norm_gain=q_norm_gain, k_norm_gain=k_norm_gain,
                  b_gate=b_gate, conv_w=conv_w, conv_b=conv_b, conv_ln_gain=conv_ln_gain,
                  conv_ln_bias=conv_ln_bias, w_attn_branch=w_attn_branch, w_conv_branch=w_conv_branch,
                  w_out=w_out, norm2_gain=norm2_gain, w_router=w_router, b_router=b_router,
                  w_gate_up=w_gate_up, b_gate_up=b_gate_up, w_down=w_down, b_down=b_down)
    p = {k: v[0] for k, v in params.items()}
    caches = (cache_kv_w128[0], cache_kv_w512[0], cache_kv_w2048[0])
    y_p, y_s, kv_p, conv_p, kv_s, conv_s = _layer(x_prompt, x_sample, caches, state_conv[0], p)
    return (y_p, y_s, kv_p[0], kv_p[1], kv_p[2], conv_p, kv_s[0], kv_s[1], kv_s[2], conv_s)
```
